```python
import jax
import jax.numpy as jnp
from jax import lax
import numpy as np

D_MODEL = 1024
BATCH = 4
SEQ = 4096
DEPTH = 4

GRID_W = 64
CTX_LEN = 256
N_MOD = 9
ATTN_HEAD_DIM = 64
ATTN_WIDTH = D_MODEL // 2
ATTN_HEADS = ATTN_WIDTH // ATTN_HEAD_DIM
RWKV_HEAD_DIM = 64
RWKV_WIDTH = D_MODEL - ATTN_WIDTH
RWKV_HEADS = RWKV_WIDTH // RWKV_HEAD_DIM
NA_ROWS = 8
NA_COLS = 16
DECAY_LORA = 64
ICLR_LORA = 64
GATE_LORA = 128
D_FF = ((8 * D_MODEL // 3 + 127) // 128) * 128
RWKV_IN = 3 * RWKV_WIDTH + DECAY_LORA + ICLR_LORA + GATE_LORA
D_IN = 3 * ATTN_WIDTH + RWKV_IN
RWKV_SPLITS = (RWKV_WIDTH, 2 * RWKV_WIDTH, 3 * RWKV_WIDTH,
               3 * RWKV_WIDTH + DECAY_LORA, 3 * RWKV_WIDTH + DECAY_LORA + ICLR_LORA)
ATTN_SCALE = ATTN_HEAD_DIM ** -0.5
RMS_EPS = 1e-6
LNX_EPS = 64e-5
L2_EPS = 1e-24
NEG_INF = -1e30

kernel_name = 'hybrid_na_rwkv7_macaron_dit'


def rms_norm(x, gain):
    xf = x.astype(jnp.float32)
    y = xf * lax.rsqrt(jnp.mean(xf * xf, axis=-1, keepdims=True) + RMS_EPS)
    return (y * gain.astype(jnp.float32)).astype(x.dtype)


def modulate(x, shift, scale):
    return x * (1 + scale) + shift


def swiglu(x, w_up, w_down):
    gate, up = jnp.split(x @ w_up, 2, axis=-1)
    return (jax.nn.silu(gate) * up) @ w_down


def attn_heads(p, q_gain, k_gain):
    b, n, _ = p.shape
    q, k, v = jnp.split(p, 3, axis=-1)
    shape = (b, n, ATTN_HEADS, ATTN_HEAD_DIM)
    q = rms_norm(q.reshape(shape), q_gain) * ATTN_SCALE
    k = rms_norm(k.reshape(shape), k_gain)
    return q, k, v.reshape(shape)


def neighbourhood_attention(q, k, v, k_ctx, v_ctx, rel_bias):
    b, t, h, d = q.shape
    rows = t // GRID_W
    kr = min(NA_ROWS, rows)
    qg = q.reshape(b, rows, GRID_W, h, d)
    kg = k.reshape(b, rows, GRID_W, h, d)
    vg = v.reshape(b, rows, GRID_W, h, d)
    cols = jnp.arange(GRID_W)
    col_start = jnp.clip(cols - NA_COLS // 2, 0, GRID_W - NA_COLS)
    in_window = (cols[None, :] >= col_start[:, None]) & (cols[None, :] < col_start[:, None] + NA_COLS)
    dc = jnp.clip(cols[None, :] - cols[:, None], -(NA_COLS - 1), NA_COLS - 1) + (NA_COLS - 1)

    def row_block(i):
        start = jnp.clip(i - kr // 2, 0, rows - kr)
        q_i = lax.dynamic_index_in_dim(qg, i, axis=1, keepdims=False)
        k_i = lax.dynamic_slice_in_dim(kg, start, kr, axis=1)
        v_i = lax.dynamic_slice_in_dim(vg, start, kr, axis=1)
        dr = start + jnp.arange(kr) - i + (NA_ROWS - 1)
        bias = rel_bias[:, dr[None, :, None], dc[:, None, :]].astype(jnp.float32)
        bias = jnp.where(in_window[None, :, None, :], bias, NEG_INF)
        s_loc = jnp.einsum('bqhd,brwhd->bhqrw', q_i, k_i).astype(jnp.float32) + bias[None]
        s_loc = s_loc.reshape(b, h, GRID_W, kr * GRID_W)
        s_ctx = jnp.einsum('bqhd,bchd->bhqc', q_i, k_ctx).astype(jnp.float32)
        p = jax.nn.softmax(jnp.concatenate([s_loc, s_ctx], axis=-1), axis=-1).astype(v.dtype)
        p_loc = p[..., :kr * GRID_W].reshape(b, h, GRID_W, kr, GRID_W)
        p_ctx = p[..., kr * GRID_W:]
        return (jnp.einsum('bhqrw,brwhd->bqhd', p_loc, v_i)
                + jnp.einsum('bhqc,bchd->bqhd', p_ctx, v_ctx))

    out = lax.map(row_block, jnp.arange(rows))
    return jnp.moveaxis(out, 0, 1).reshape(b, t, h * d)


def context_attention(q, k, v):
    b, n, h, d = q.shape
    s = jnp.einsum('bqhd,bkhd->bhqk', q, k).astype(jnp.float32)
    p = jax.nn.softmax(s, axis=-1).astype(v.dtype)
    return jnp.einsum('bhqk,bkhd->bqhd', p, v).reshape(b, n, h * d)


def centred_shift(p):
    padded = jnp.pad(p, ((0, 0), (1, 1), (0, 0)))
    return 0.5 * (padded[:, :-2] + padded[:, 2:])


def rwkv_prepare(p, shift_mu, decay_w0, decay_w2, iclr_a0, iclr_a2, gate_g2, key_k, key_a):
    b, n, _ = p.shape
    p = p + shift_mu * (centred_shift(p) - p)
    r, k, v, dw, da, dg = jnp.split(p, RWKV_SPLITS, axis=-1)
    heads = lambda t: t.astype(jnp.float32).reshape(b, n, RWKV_HEADS, RWKV_HEAD_DIM)
    g = jax.nn.sigmoid(dg) @ gate_g2
    kk = heads(k * key_k)
    kk = kk * lax.rsqrt(jnp.maximum(jnp.sum(kk * kk, axis=-1, keepdims=True), L2_EPS))
    per_dir = []
    for d in range(2):
        w_log = -jax.nn.softplus(-(decay_w0[d] + jnp.tanh(dw) @ decay_w2[d])) - 0.5
        a = jax.nn.sigmoid(iclr_a0[d] + da @ iclr_a2[d])
        k_d = k * (1 + (a - 1) * key_a)
        decay = jnp.exp(-jnp.exp(w_log.astype(jnp.float32)))
        per_dir.append((heads(decay), heads(k_d), heads(a)))
    return heads(r), heads(v), kk, g, per_dir


def rwkv_scan(state0, r, decay, k, v, kk, a, reverse):
    emit = r is not None

    def step(S, inp):
        r_t, w_t, k_t, v_t, kk_t, a_t = inp
        removal = jnp.einsum('bhvk,bhk->bhv', S, kk_t)
        S = (S * w_t[:, :, None, :]
             - removal[..., None] * (kk_t * a_t)[:, :, None, :]
             + v_t[..., None] * k_t[:, :, None, :])
        out = jnp.einsum('bhvk,bhk->bhv', S, r_t) if emit else None
        return S, out

    xs = jax.tree_util.tree_map(lambda t: jnp.moveaxis(t, 1, 0), (r, decay, k, v, kk, a))
    S, out = lax.scan(step, state0, xs, reverse=reverse)
    return S, (jnp.moveaxis(out, 0, 1) if emit else None)


def rwkv_readout(o_fwd, o_bwd, r, k_fwd, k_bwd, v, g, bonus_u, lnx_gain, lnx_bias):
    b, n, h, d = o_fwd.shape
    o = o_fwd + o_bwd
    mean = jnp.mean(o, axis=-1, keepdims=True)
    var = jnp.mean(jnp.square(o - mean), axis=-1, keepdims=True)
    o = ((o - mean) * lax.rsqrt(var + LNX_EPS)).reshape(b, n, h * d) * lnx_gain + lnx_bias
    bonus = jnp.sum(r * (k_fwd + k_bwd) * bonus_u, axis=-1, keepdims=True) * v
    return (o + bonus.reshape(b, n, h * d)) * g


def token_mixing(nx, ns, w_in, q_gain, k_gain, na_bias, shift_mu, decay_w0, decay_w2,
                 iclr_a0, iclr_a2, gate_g2, key_k, key_a, bonus_u, lnx_gain, lnx_bias,
                 w_out, emit_ctx):
    px = nx @ w_in
    ps = ns @ w_in
    q_x, k_x, v_x = attn_heads(px[..., :3 * ATTN_WIDTH], q_gain, k_gain)
    q_s, k_s, v_s = attn_heads(ps[..., :3 * ATTN_WIDTH], q_gain, k_gain)
    attn_x = neighbourhood_attention(q_x, k_x, v_x, k_s, v_s, na_bias)
    rwkv_args = (shift_mu, decay_w0, decay_w2, iclr_a0, iclr_a2, gate_g2, key_k, key_a)
    r_x, vr_x, kk_x, g_x, dirs_x = rwkv_prepare(px[..., 3 * ATTN_WIDTH:], *rwkv_args)
    r_s, vr_s, kk_s, g_s, dirs_s = rwkv_prepare(ps[..., 3 * ATTN_WIDTH:], *rwkv_args)
    state0 = jnp.zeros((ns.shape[0], RWKV_HEADS, RWKV_HEAD_DIM, RWKV_HEAD_DIM), jnp.float32)
    outs_x, outs_s = [], []
    for d in range(2):
        decay_s, kd_s, a_s = dirs_s[d]
        state_ctx, o_s = rwkv_scan(state0, r_s if emit_ctx else None, decay_s, kd_s, vr_s, kk_s, a_s, d == 1)
        decay_x, kd_x, a_x = dirs_x[d]
        _, o_x = rwkv_scan(state_ctx, r_x, decay_x, kd_x, vr_x, kk_x, a_x, d == 1)
        outs_x.append(o_x)
        outs_s.append(o_s)
    rwkv_x = rwkv_readout(outs_x[0], outs_x[1], r_x, dirs_x[0][1], dirs_x[1][1], vr_x, g_x,
                          bonus_u, lnx_gain, lnx_bias)
    y_x = jnp.concatenate([attn_x, rwkv_x.astype(attn_x.dtype)], axis=-1) @ w_out
    if not emit_ctx:
        return y_x, None
    attn_s = context_attention(q_s, k_s, v_s)
    rwkv_s = rwkv_readout(outs_s[0], outs_s[1], r_s, dirs_s[0][1], dirs_s[1][1], vr_s, g_s,
                          bonus_u, lnx_gain, lnx_bias)
    y_s = jnp.concatenate([attn_s, rwkv_s.astype(attn_s.dtype)], axis=-1) @ w_out
    return y_x, y_s


def setup_inputs(seed: int = 0) -> dict:
    key = jax.random.key(seed)
    ks = jax.random.split(key, 25)
    L, D, f32 = DEPTH, D_MODEL, jnp.float32

    def nrm(k, shape, scale):
        return scale * jax.random.normal(k, shape, f32)

    def uni(k, shape, lo, hi):
        return jax.random.uniform(k, shape, f32, lo, hi)

    return {
        'x': nrm(ks[0], (BATCH, SEQ, D), 1.0),
        'c': nrm(ks[1], (BATCH, D), 1.0),
        'ctx': nrm(ks[2], (BATCH, CTX_LEN, D), 1.0),
        'c_ctx': nrm(ks[3], (D,), 1.0),
        'w_mod': nrm(ks[4], (L, D, N_MOD * D), 0.5 * D ** -0.5),
        'b_mod': nrm(ks[5], (L, N_MOD * D), 0.02),
        'norm_gain': 1.0 + nrm(ks[6], (L, 3, D), 0.05),
        'ffn_up': nrm(ks[7], (L, 2, D, 2 * D_FF), D ** -0.5),
        'ffn_down': nrm(ks[8], (L, 2, D_FF, D), D_FF ** -0.5),
        'w_in': nrm(ks[9], (L, D, D_IN), D ** -0.5),
        'q_gain': 1.0 + nrm(ks[10], (L, ATTN_HEAD_DIM), 0.05),
        'k_gain': 1.0 + nrm(ks[11], (L, ATTN_HEAD_DIM), 0.05),
        'na_bias': nrm(ks[12], (L, ATTN_HEADS, 2 * NA_ROWS - 1, 2 * NA_COLS - 1), 0.1),
        'shift_mu': uni(ks[13], (L, RWKV_IN), 0.0, 1.0),
        'decay_w0': uni(ks[14], (L, 2, RWKV_WIDTH), -6.0, -1.0),
        'decay_w2': nrm(ks[15], (L, 2, DECAY_LORA, RWKV_WIDTH), 0.5 * DECAY_LORA ** -0.5),
        'iclr_a0': nrm(ks[16], (L, 2, RWKV_WIDTH), 0.1),
        'iclr_a2': nrm(ks[17], (L, 2, ICLR_LORA, RWKV_WIDTH), ICLR_LORA ** -0.5),
        'gate_g2': nrm(ks[18], (L, GATE_LORA, RWKV_WIDTH), GATE_LORA ** -0.5),
        'key_k': 0.85 + nrm(ks[19], (L, RWKV_WIDTH), 0.05),
        'key_a': 1.0 + nrm(ks[20], (L, RWKV_WIDTH), 0.05),
        'bonus_u': nrm(ks[21], (L, RWKV_HEADS, RWKV_HEAD_DIM), 0.1),
        'lnx_gain': 1.0 + nrm(ks[22], (L, RWKV_WIDTH), 0.05),
        'lnx_bias': nrm(ks[23], (L, RWKV_WIDTH), 0.02),
        'w_out': nrm(ks[24], (L, D, D), D ** -0.5),
    }


def reference(x, c, ctx, c_ctx, w_mod, b_mod, norm_gain, ffn_up, ffn_down, w_in, q_gain,
              k_gain, na_bias, shift_mu, decay_w0, decay_w2, iclr_a0, iclr_a2, gate_g2,
              key_k, key_a, bonus_u, lnx_gain, lnx_bias, w_out):
    s = ctx
    silu_c = jax.nn.silu(c)
    silu_cc = jax.nn.silu(c_ctx)[None, :]
    for l in range(DEPTH):
        last = l == DEPTH - 1
        mx = jnp.split((silu_c @ w_mod[l] + b_mod[l])[:, None, :], N_MOD, axis=-1)
        ms = jnp.split((silu_cc @ w_mod[l] + b_mod[l])[:, None, :], N_MOD, axis=-1)
        x = x + 0.5 * mx[2] * swiglu(modulate(rms_norm(x, norm_gain[l, 0]), mx[0], mx[1]),
                                     ffn_up[l, 0], ffn_down[l, 0])
        s = s + 0.5 * ms[2] * swiglu(modulate(rms_norm(s, norm_gain[l, 0]), ms[0], ms[1]),
                                     ffn_up[l, 0], ffn_down[l, 0])
        nx = modulate(rms_norm(x, norm_gain[l, 1]), mx[3], mx[4])
        ns = modulate(rms_norm(s, norm_gain[l, 1]), ms[3], ms[4])
        y_x, y_s = token_mixing(nx, ns, w_in[l], q_gain[l], k_gain[l], na_bias[l], shift_mu[l],
                                decay_w0[l], decay_w2[l], iclr_a0[l], iclr_a2[l], gate_g2[l],
                                key_k[l], key_a[l], bonus_u[l], lnx_gain[l], lnx_bias[l],
                                w_out[l], not last)
        x = x + mx[5] * y_x
        x = x + 0.5 * mx[8] * swiglu(modulate(rms_norm(x, norm_gain[l, 2]), mx[6], mx[7]),
                                     ffn_up[l, 1], ffn_down[l, 1])
        if not last:
            s = s + ms[5] * y_s
            s = s + 0.5 * ms[8] * swiglu(modulate(rms_norm(s, norm_gain[l, 2]), ms[6], ms[7]),
                                         ffn_up[l, 1], ffn_down[l, 1])
    return x
```

```python
import functools

import jax
import jax.numpy as jnp
import numpy as np
from jax import lax
from jax.experimental import pallas as pl
from jax.experimental.pallas import tpu as pltpu

F32 = jnp.float32
BF16 = jnp.bfloat16
HIGHEST = lax.Precision.HIGHEST

GRID_W = 64
NA_ROWS = 8
NA_COLS = 16
HEAD_DIM = 64
LANES = 128
TM = 256
CHUNK = 64
Q_ROWS = 4
KEY_ROWS = Q_ROWS + NA_ROWS - 1
RMS_EPS = 1e-6
LNX_EPS = 64e-5
L2_EPS = 1e-24
NEG_INF = -1e30
ATTN_SCALE = HEAD_DIM ** -0.5
VMEM_LIMIT = 56 * 1024 * 1024


def _params(*sem):
    return pltpu.CompilerParams(dimension_semantics=sem, vmem_limit_bytes=VMEM_LIMIT)


def _bdot(a, b):
    return jnp.dot(a.astype(BF16), b.astype(BF16), preferred_element_type=F32)


def _hdot(a, b):
    return jnp.dot(a, b, precision=HIGHEST, preferred_element_type=F32)


def _hdot_nt(a, b):
    return lax.dot_general(a, b, (((1,), (1,)), ((), ())), precision=HIGHEST,
                           preferred_element_type=F32)


def _hdot_tn(a, b):
    return lax.dot_general(a, b, (((0,), (0,)), ((), ())), precision=HIGHEST,
                           preferred_element_type=F32)


def _sigmoid(x):
    return 1.0 / (1.0 + jnp.exp(-x))


def _head0_mask(shape):
    return lax.broadcasted_iota(jnp.int32, shape, len(shape) - 1) < HEAD_DIM


def _head_sum(z):
    m0 = _head0_mask(z.shape)
    s0 = jnp.sum(jnp.where(m0, z, 0.0), axis=-1, keepdims=True)
    s1 = jnp.sum(jnp.where(m0, 0.0, z), axis=-1, keepdims=True)
    return jnp.where(m0, s0, s1)


def _norm_modulate(x, gain, shift, scale):
    y = x * lax.rsqrt(jnp.mean(x * x, axis=-1, keepdims=True) + RMS_EPS)
    return (y * gain) * (1.0 + scale) + shift


def _mod_kernel(c_ref, w_ref, b_ref, o_ref):
    c = c_ref[...]
    o_ref[0] = _bdot(c * _sigmoid(c), w_ref[0]) + b_ref[0]


def _modulation(cvec, w_mod, b_mod):
    n_layer, d, nd = w_mod.shape
    return pl.pallas_call(
        _mod_kernel,
        grid=(n_layer, nd // d),
        in_specs=[pl.BlockSpec((8, d), lambda l, j: (0, 0)),
                  pl.BlockSpec((1, d, d), lambda l, j: (l, 0, j)),
                  pl.BlockSpec((1, 1, d), lambda l, j: (l, 0, j))],
        out_specs=pl.BlockSpec((1, 8, d), lambda l, j: (l, 0, j)),
        out_shape=jax.ShapeDtypeStruct((n_layer, 8, nd), F32),
        compiler_params=_params("parallel", "parallel"),
    )(cvec, w_mod, b_mod.reshape(n_layer, 1, nd))


def _ffn_apply(x, mod, sub, gain, wup_ref, wdown_ref, n_chunk):
    d_ff = wdown_ref.shape[0]
    ck = d_ff // n_chunk
    xn = _norm_modulate(x, gain, mod[3 * sub:3 * sub + 1], mod[3 * sub + 1:3 * sub + 2]).astype(BF16)
    acc = jnp.zeros(x.shape, F32)
    for c in range(n_chunk):
        gate = jnp.dot(xn, wup_ref[:, c * ck:(c + 1) * ck], preferred_element_type=F32)
        up = jnp.dot(xn, wup_ref[:, d_ff + c * ck:d_ff + (c + 1) * ck], preferred_element_type=F32)
        h = (gate * _sigmoid(gate) * up).astype(BF16)
        acc = acc + jnp.dot(h, wdown_ref[c * ck:(c + 1) * ck, :], preferred_element_type=F32)
    return x + (0.5 * mod[3 * sub + 2:3 * sub + 3]) * acc


def _ffn_kernel(x_ref, mod_ref, gain_ref, wup_ref, wdown_ref, o_ref, *, sub, n_chunk):
    o_ref[0] = _ffn_apply(x_ref[0], mod_ref[0, 0], sub, gain_ref[...], wup_ref, wdown_ref, n_chunk)


def _row_specs(d):
    return pl.BlockSpec((1, TM, d), lambda b, t: (b, t, 0))


def _mod_spec(d):
    return pl.BlockSpec((1, 1, 9, d), lambda b, t: (b, jnp.minimum(t, 1), 0, 0))


def _const_spec(shape):
    return pl.BlockSpec(shape, lambda b, t: (0,) * len(shape), pipeline_mode=pl.Buffered(1))


def _ffn(x, modtab, gain, wup, wdown, sub):
    nb, tt, d = x.shape
    d_ff = wdown.shape[0]
    return pl.pallas_call(
        functools.partial(_ffn_kernel, sub=sub, n_chunk=2),
        grid=(nb, tt // TM),
        in_specs=[_row_specs(d), _mod_spec(d), _const_spec((1, d)),
                  _const_spec((d, 2 * d_ff)), _const_spec((d_ff, d))],
        out_specs=_row_specs(d),
        out_shape=jax.ShapeDtypeStruct(x.shape, F32),
        compiler_params=_params("parallel", "parallel"),
    )(x, modtab, gain, wup, wdown)


def _inproj_kernel(x_ref, mod_ref, gain_ref, w_ref, qg_ref, kg_ref,
                   q_ref, k_ref, v_ref, pr_ref, *, attn_w):
    mod = mod_ref[0, 0]
    xn = _norm_modulate(x_ref[0], gain_ref[...], mod[3:4], mod[4:5]).astype(BF16)
    p = jnp.dot(xn, w_ref[...], preferred_element_type=F32)
    for col in range(attn_w // LANES):
        sl = slice(col * LANES, (col + 1) * LANES)
        qc = p[:, sl]
        kc = p[:, attn_w + col * LANES:attn_w + (col + 1) * LANES]
        qn = qc * lax.rsqrt(_head_sum(qc * qc) * (1.0 / HEAD_DIM) + RMS_EPS) * qg_ref[...]
        kn = kc * lax.rsqrt(_head_sum(kc * kc) * (1.0 / HEAD_DIM) + RMS_EPS) * kg_ref[...]
        q_ref[0, :, sl] = (qn * ATTN_SCALE).astype(BF16)
        k_ref[0, :, sl] = kn.astype(BF16)
    v_ref[0] = p[:, 2 * attn_w:3 * attn_w].astype(BF16)
    pr_ref[0] = p[:, 3 * attn_w:]


def _inproj(x, modtab, gain, w_in, q_gain2, k_gain2, attn_w):
    nb, tt, d = x.shape
    d_in = w_in.shape[1]
    rw = d_in - 3 * attn_w
    return pl.pallas_call(
        functools.partial(_inproj_kernel, attn_w=attn_w),
        grid=(nb, tt // TM),
        in_specs=[_row_specs(d), _mod_spec(d), _const_spec((1, d)), _const_spec((d, d_in)),
                  _const_spec((1, LANES)), _const_spec((1, LANES))],
        out_specs=[_row_specs(attn_w), _row_specs(attn_w), _row_specs(attn_w), _row_specs(rw)],
        out_shape=[jax.ShapeDtypeStruct((nb, tt, attn_w), BF16)] * 3
        + [jax.ShapeDtypeStruct((nb, tt, rw), F32)],
        compiler_params=_params("parallel", "parallel"),
    )(x, modtab, gain, w_in, q_gain2, k_gain2)


def _attn_kernel(q_ref, k_ref, v_ref, bias_ref, o_ref, *, ctx_len, n_blk):
    j = pl.program_id(2)
    q = q_ref[0]
    m0 = _head0_mask(q.shape)
    zero = jnp.zeros_like(q)
    k_ctx = k_ref[0, 0:ctx_len, :]
    v_ctx = v_ref[0, 0:ctx_len, :]
    nt = (((1,), (1,)), ((), ()))

    def head_q(hh):
        return jnp.where(m0, q, zero) if hh == 0 else jnp.where(m0, zero, q)

    @pl.when(j == 0)
    def _():
        outs = []
        for hh in range(2):
            s = lax.dot_general(head_q(hh), k_ctx, nt, preferred_element_type=F32)
            p = jnp.exp(s - jnp.max(s, axis=-1, keepdims=True))
            o = jnp.dot(p.astype(BF16), v_ctx, preferred_element_type=F32)
            outs.append(o * (1.0 / jnp.sum(p, axis=-1, keepdims=True)))
        o_ref[0] = jnp.where(m0, outs[0], outs[1]).astype(BF16)

    @pl.when(j > 0)
    def _():
        key_row0 = jnp.clip(Q_ROWS * (j - 1) - NA_ROWS // 2, 0, n_blk * Q_ROWS - KEY_ROWS)
        row0 = pl.multiple_of(ctx_len + key_row0 * GRID_W, GRID_W)
        k_loc = k_ref[0, pl.ds(row0, KEY_ROWS * GRID_W), :]
        v_loc = v_ref[0, pl.ds(row0, KEY_ROWS * GRID_W), :]
        outs = []
        for hh in range(2):
            qh = head_q(hh)
            s_loc = lax.dot_general(qh, k_loc, nt, preferred_element_type=F32) + bias_ref[0, hh]
            s_ctx = lax.dot_general(qh, k_ctx, nt, preferred_element_type=F32)
            mx = jnp.maximum(jnp.max(s_loc, axis=-1, keepdims=True),
                             jnp.max(s_ctx, axis=-1, keepdims=True))
            p_loc = jnp.exp(s_loc - mx)
            p_ctx = jnp.exp(s_ctx - mx)
            den = jnp.sum(p_loc, axis=-1, keepdims=True) + jnp.sum(p_ctx, axis=-1, keepdims=True)
            o = (jnp.dot(p_loc.astype(BF16), v_loc, preferred_element_type=F32)
                 + jnp.dot(p_ctx.astype(BF16), v_ctx, preferred_element_type=F32))
            outs.append(o * (1.0 / den))
        o_ref[0] = jnp.where(m0, outs[0], outs[1]).astype(BF16)


def _attention(q, k, v, bias, ctx_len):
    nb, tt, w = q.shape
    n_blk = (tt - ctx_len) // (Q_ROWS * GRID_W)
    q_rows = Q_ROWS * GRID_W
    assert q_rows == ctx_len == TM

    def bias_map(hp, b, j):
        pat = jnp.where(j <= 1, 0, jnp.where(j == n_blk, 2, 1))
        return (pat, hp, 0, 0)

    return pl.pallas_call(
        functools.partial(_attn_kernel, ctx_len=ctx_len, n_blk=n_blk),
        grid=(w // LANES, nb, n_blk + 1),
        in_specs=[pl.BlockSpec((1, q_rows, LANES), lambda hp, b, j: (b, j, hp)),
                  pl.BlockSpec((1, tt, LANES), lambda hp, b, j: (b, 0, hp)),
                  pl.BlockSpec((1, tt, LANES), lambda hp, b, j: (b, 0, hp)),
                  pl.BlockSpec((1, 2, q_rows, KEY_ROWS * GRID_W), bias_map)],
        out_specs=pl.BlockSpec((1, q_rows, LANES), lambda hp, b, j: (b, j, hp)),
        out_shape=jax.ShapeDtypeStruct((nb, tt, w), BF16),
        compiler_params=_params("parallel", "parallel", "arbitrary"),
    )(q, k, v, bias)


def _attention_bias(na_bias, n_blk):
    rows = n_blk * Q_ROWS
    cols = np.arange(GRID_W)
    col_start = np.clip(cols - NA_COLS // 2, 0, GRID_W - NA_COLS)
    in_window = (cols[None, :] >= col_start[:, None]) & (cols[None, :] < col_start[:, None] + NA_COLS)
    dc = np.clip(cols[None, :] - cols[:, None], -(NA_COLS - 1), NA_COLS - 1) + (NA_COLS - 1)
    dr = np.zeros((3, Q_ROWS, KEY_ROWS), np.int32)
    valid = np.zeros((3, Q_ROWS, KEY_ROWS), bool)
    for pat, blk in enumerate((0, 1, n_blk - 1)):
        key_row0 = int(np.clip(Q_ROWS * blk - NA_ROWS // 2, 0, rows - KEY_ROWS))
        for m in range(Q_ROWS):
            qi = Q_ROWS * blk + m
            start = int(np.clip(qi - NA_ROWS // 2, 0, rows - NA_ROWS))
            for kl in range(KEY_ROWS):
                kr = key_row0 + kl
                valid[pat, m, kl] = start <= kr < start + NA_ROWS
                dr[pat, m, kl] = np.clip(kr - qi + NA_ROWS - 1, 0, 2 * NA_ROWS - 2)
    g = na_bias[:, dr[:, :, None, :, None], dc[None, None, :, None, :]]
    keep = valid[:, :, None, :, None] & in_window[None, None, :, None, :]
    g = jnp.where(keep[None], g.astype(F32), NEG_INF)
    g = jnp.transpose(g, (1, 0, 2, 3, 4, 5))
    return g.reshape(3, na_bias.shape[0], Q_ROWS * GRID_W, KEY_ROWS * GRID_W)


def _prep_kernel(p_ref, hp_ref, hn_ref, mu_ref, w0_ref, w2_ref, a0_ref, a2_ref, g2_ref,
                 kk_ref, ka_ref, u_ref,
                 r_ref, v_ref, kkn_ref, g_ref, bon_ref, lw_ref, kd_ref, bb_ref, *, hw, n_tile):
    t = pl.program_id(1)
    p = p_ref[0]
    rows = lax.broadcasted_iota(jnp.int32, (TM, 1), 0)
    prev_ok = (t >= 2).astype(F32)
    next_ok = jnp.logical_and(t >= 1, t <= n_tile - 2).astype(F32)
    prev = jnp.where(rows == 0, hp_ref[0, 7:8, :] * prev_ok, pltpu.roll(p, 1, 0))
    nxt = jnp.where(rows == TM - 1, hn_ref[0, 0:1, :] * next_ok, pltpu.roll(p, TM - 1, 0))
    p = p + mu_ref[...] * (0.5 * (prev + nxt) - p)
    r = p[:, 0:hw]
    k = p[:, hw:2 * hw]
    v = p[:, 2 * hw:3 * hw]
    lora = p[:, 3 * hw:3 * hw + LANES]
    dg = p[:, 3 * hw + LANES:3 * hw + 2 * LANES]
    g_ref[0] = _bdot(_sigmoid(dg), g2_ref[...])
    r_ref[0] = r
    v_ref[0] = v
    tl = jnp.tanh(lora)
    ksum = jnp.zeros_like(k)
    kks = []
    for col in range(hw // LANES):
        sl = slice(col * LANES, (col + 1) * LANES)
        kc = k[:, sl] * kk_ref[:, sl]
        kks.append(kc * lax.rsqrt(jnp.maximum(_head_sum(kc * kc), L2_EPS)))
    kkn = jnp.concatenate(kks, axis=1)
    kkn_ref[0] = kkn
    for d in range(2):
        z = w0_ref[d:d + 1, :] + _bdot(tl, w2_ref[d])
        lw_ref[d, 0] = -np.float32(np.exp(-0.5)) * _sigmoid(z)
        a = _sigmoid(a0_ref[d:d + 1, :] + _bdot(lora, a2_ref[d]))
        kd = k * (1.0 + (a - 1.0) * ka_ref[...])
        kd_ref[d, 0] = kd
        bb_ref[d, 0] = kkn * a
        ksum = ksum + kd
    bon = r * ksum * u_ref[...]
    for col in range(hw // LANES):
        sl = slice(col * LANES, (col + 1) * LANES)
        bon_ref[0, :, sl] = _head_sum(bon[:, sl]) * v[:, sl]


def _rwkv_prepare(pr, shift_mu, w0, w2p, a0, a2p, g2, key_k, key_a, bonus_u, hw):
    nb, tt, rw = pr.shape
    n_tile = tt // TM
    sub = TM // 8
    tok = jax.ShapeDtypeStruct((nb, tt, hw), F32)
    tok2 = jax.ShapeDtypeStruct((2, nb, tt, hw), F32)
    tok_spec = _row_specs(hw)
    tok2_spec = pl.BlockSpec((2, 1, TM, hw), lambda b, t: (0, b, t, 0))
    return pl.pallas_call(
        functools.partial(_prep_kernel, hw=hw, n_tile=n_tile),
        grid=(nb, n_tile),
        in_specs=[_row_specs(rw),
                  pl.BlockSpec((1, 8, rw), lambda b, t: (b, jnp.maximum(t * sub - 1, 0), 0)),
                  pl.BlockSpec((1, 8, rw), lambda b, t: (b, jnp.minimum((t + 1) * sub, tt // 8 - 1), 0)),
                  _const_spec((1, rw)), _const_spec((2, hw)), _const_spec((2, LANES, hw)),
                  _const_spec((2, hw)), _const_spec((2, LANES, hw)), _const_spec((LANES, hw)),
                  _const_spec((1, hw)), _const_spec((1, hw)), _const_spec((1, hw))],
        out_specs=[tok_spec] * 5 + [tok2_spec] * 3,
        out_shape=[tok] * 5 + [tok2] * 3,
        compiler_params=_params("parallel", "parallel"),
    )(pr, pr, pr, shift_mu, w0, w2p, a0, a2p, g2, key_k, key_a, bonus_u)


def _stack_heads(x):
    m0 = _head0_mask(x.shape)
    zero = jnp.zeros_like(x)
    return jnp.concatenate([jnp.where(m0, x, zero), jnp.where(m0, zero, x)], axis=0)


def _scan_kernel(r_ref, v_ref, kk_ref, lw_ref, kd_ref, bb_ref, o_ref, st_ref, *, hw):
    d = pl.program_id(0)
    c = pl.program_id(2)
    n_pair = hw // LANES
    c2 = 2 * CHUNK

    @pl.when(c == 0)
    def _():
        st_ref[...] = jnp.zeros(st_ref.shape, F32)

    sgn = jnp.where(d == 0, 1, -1)
    ti = lax.broadcasted_iota(jnp.int32, (CHUNK, CHUNK), 0)
    tj = lax.broadcasted_iota(jnp.int32, (CHUNK, CHUNK), 1)
    incl1 = ((tj - ti) * sgn <= 0).astype(F32)
    ri = lax.broadcasted_iota(jnp.int32, (c2, c2), 0)
    rj = lax.broadcasted_iota(jnp.int32, (c2, c2), 1)
    same = (ri // CHUNK) == (rj // CHUNK)
    before = ((rj % CHUNK) - (ri % CHUNK)) * sgn
    incl = jnp.logical_and(same, before <= 0).astype(F32)
    strict = jnp.logical_and(same, before < 0).astype(F32)
    eye = (ri == rj).astype(F32)
    li = lax.broadcasted_iota(jnp.int32, (LANES, LANES), 0)
    lj = lax.broadcasted_iota(jnp.int32, (LANES, LANES), 1)
    eye_l = (li == lj).astype(F32)

    lw = lw_ref[0, 0]
    cum = _hdot(incl1, lw)
    cum_prev = cum - lw
    tot = jnp.sum(lw, axis=0, keepdims=True)
    mid = 0.5 * tot
    r, v, kk = r_ref[0], v_ref[0], kk_ref[0]
    kd, bb = kd_ref[0, 0], bb_ref[0, 0]
    e_in = jnp.exp(cum_prev - mid)
    e_out = jnp.exp(cum - mid)
    e_inv = jnp.exp(mid - cum)
    e_end = jnp.exp(tot - cum)
    a_hat, r_hat = kk * e_in, r * e_out
    k_til, b_til = kd * e_inv, bb * e_inv
    a_abs, r_abs = kk * jnp.exp(cum_prev), r * jnp.exp(cum)
    k_end, b_end = kd * e_end, bb * e_end
    e_tot = jnp.exp(tot)

    for pr in range(n_pair):
        sl = slice(pr * LANES, (pr + 1) * LANES)
        a_s, r_s = _stack_heads(a_hat[:, sl]), _stack_heads(r_hat[:, sl])
        k_s, b_s = _stack_heads(k_til[:, sl]), _stack_heads(b_til[:, sl])
        v_s = _stack_heads(v[:, sl])
        low = strict * _hdot_nt(a_s, b_s)
        m_k = strict * _hdot_nt(a_s, k_s)
        m_rk = incl * _hdot_nt(r_s, k_s)
        m_rb = incl * _hdot_nt(r_s, b_s)
        t_inv = eye - low
        pw = low
        for _ in range(int(np.log2(CHUNK)) - 1):
            pw = _hdot(pw, pw)
            t_inv = t_inv + _hdot(pw, t_inv)
        au = _hdot(t_inv, jnp.concatenate([_stack_heads(a_abs[:, sl]), _hdot(m_k, v_s)], axis=1))
        qo = (jnp.concatenate([_stack_heads(r_abs[:, sl]), _hdot(m_rk, v_s)], axis=1)
              - _hdot(m_rb, au))
        ba = _hdot_tn(_stack_heads(b_end[:, sl]), au)
        g_mat = eye_l * e_tot[:, sl] - ba[:, :LANES]
        h_mat = _hdot_tn(_stack_heads(k_end[:, sl]), v_s) - ba[:, LANES:]
        st = st_ref[pr]
        o_s = _hdot(qo[:, :LANES], st) + qo[:, LANES:]
        o_ref[0, 0, :, sl] = o_s[:CHUNK] + o_s[CHUNK:]
        st_ref[pr] = _hdot(g_mat, st) + h_mat


def _rwkv_scan(r, v, kk, lw, kd, bb, ctx_len):
    nb, tt, hw = r.shape
    n_chunk = tt // CHUNK
    n_ctx = ctx_len // CHUNK

    def tok_chunk(d, c):
        back = jnp.where(c < n_ctx, n_ctx - 1 - c, n_chunk + n_ctx - 1 - c)
        return jnp.where(d == 0, c, back)

    shared = pl.BlockSpec((1, CHUNK, hw), lambda d, b, c: (b, tok_chunk(d, c), 0))
    per_dir = pl.BlockSpec((1, 1, CHUNK, hw), lambda d, b, c: (d, b, tok_chunk(d, c), 0))
    return pl.pallas_call(
        functools.partial(_scan_kernel, hw=hw),
        grid=(2, nb, n_chunk),
        in_specs=[shared, shared, shared, per_dir, per_dir, per_dir],
        out_specs=per_dir,
        out_shape=jax.ShapeDtypeStruct((2, nb, tt, hw), F32),
        scratch_shapes=[pltpu.VMEM((hw // LANES, LANES, LANES), F32)],
        compiler_params=_params("parallel", "parallel", "arbitrary"),
    )(r, v, kk, lw, kd, bb)


def _outproj_kernel(x_ref, mod_ref, attn_ref, of_ref, ob_ref, bon_ref, g_ref, lg_ref, lb_ref,
                    w_ref, o_ref, *, hw):
    o = of_ref[0, 0] + ob_ref[0, 0]
    cols = []
    for col in range(hw // LANES):
        sl = slice(col * LANES, (col + 1) * LANES)
        oc = o[:, sl]
        dev = oc - _head_sum(oc) * (1.0 / HEAD_DIM)
        var = _head_sum(dev * dev) * (1.0 / HEAD_DIM)
        cols.append(dev * lax.rsqrt(var + LNX_EPS))
    on = jnp.concatenate(cols, axis=1) * lg_ref[...] + lb_ref[...]
    rw = ((on + bon_ref[0]) * g_ref[0]).astype(BF16)
    aw = attn_ref.shape[2]
    y = (jnp.dot(attn_ref[0], w_ref[0:aw, :], preferred_element_type=F32)
         + jnp.dot(rw, w_ref[aw:, :], preferred_element_type=F32))
    o_ref[0] = x_ref[0] + mod_ref[0, 0][5:6] * y


def _outproj(x, modtab, attn, o_dirs, bon, g, lnx_gain, lnx_bias, w_out):
    nb, tt, d = x.shape
    hw = bon.shape[2]
    aw = attn.shape[2]
    return pl.pallas_call(
        functools.partial(_outproj_kernel, hw=hw),
        grid=(nb, tt // TM),
        in_specs=[_row_specs(d), _mod_spec(d), _row_specs(aw),
                  pl.BlockSpec((1, 1, TM, hw), lambda b, t: (0, b, t, 0)),
                  pl.BlockSpec((1, 1, TM, hw), lambda b, t: (1, b, t, 0)),
                  _row_specs(hw), _row_specs(hw), _const_spec((1, hw)), _const_spec((1, hw)),
                  _const_spec((d, d))],
        out_specs=_row_specs(d),
        out_shape=jax.ShapeDtypeStruct(x.shape, F32),
        compiler_params=_params("parallel", "parallel"),
    )(x, modtab, attn, o_dirs, o_dirs, bon, g, lnx_gain, lnx_bias, w_out)


def kernel(x, c, ctx, c_ctx, w_mod, b_mod, norm_gain, ffn_up, ffn_down, w_in, q_gain, k_gain,
           na_bias, shift_mu, decay_w0, decay_w2, iclr_a0, iclr_a2, gate_g2, key_k, key_a,
           bonus_u, lnx_gain, lnx_bias, w_out):
    nb, seq, d = x.shape
    ctx_len = ctx.shape[1]
    depth = w_mod.shape[0]
    n_heads = na_bias.shape[1]
    attn_w = n_heads * HEAD_DIM
    hw = decay_w0.shape[2]
    lora_w = decay_w2.shape[2]
    assert ctx_len == TM and seq % (Q_ROWS * GRID_W) == 0 and attn_w % LANES == 0 and hw % LANES == 0
    assert lora_w + iclr_a2.shape[2] == LANES and gate_g2.shape[1] == LANES and nb + 1 <= 8

    cvec = jnp.zeros((8, d), F32).at[:nb].set(c).at[nb].set(c_ctx)
    mods = _modulation(cvec, w_mod, b_mod).reshape(depth, 8, 9, d)
    modtab = jnp.stack([jnp.broadcast_to(mods[:, nb:nb + 1], (depth, nb, 9, d)), mods[:, :nb]], axis=2)

    up_b, down_b = ffn_up.astype(BF16), ffn_down.astype(BF16)
    w_in_b, w_out_b = w_in.astype(BF16), w_out.astype(BF16)
    zpad = jnp.zeros_like(decay_w2)
    w2p = jnp.concatenate([decay_w2, zpad], axis=2)
    a2p = jnp.concatenate([zpad, iclr_a2], axis=2)
    n_blk = seq // (Q_ROWS * GRID_W)

    xs = jnp.concatenate([ctx, x], axis=1)
    for l in range(depth):
        mt = modtab[l]
        gains = norm_gain[l].reshape(3, 1, d)
        xs = _ffn(xs, mt, gains[0], up_b[l, 0], down_b[l, 0], 0)
        q, k, v, pr = _inproj(xs, mt, gains[1], w_in_b[l],
                              jnp.tile(q_gain[l], 2)[None], jnp.tile(k_gain[l], 2)[None], attn_w)
        attn = _attention(q, k, v, _attention_bias(na_bias[l], n_blk), ctx_len)
        r, vr, kk, g, bon, lw, kd, bb = _rwkv_prepare(
            pr, shift_mu[l][None], decay_w0[l], w2p[l], iclr_a0[l], a2p[l], gate_g2[l],
            key_k[l][None], key_a[l][None], bonus_u[l].reshape(1, hw), hw)
        o_dirs = _rwkv_scan(r, vr, kk, lw, kd, bb, ctx_len)
        xs = _outproj(xs, mt, attn, o_dirs, bon, g, lnx_gain[l][None], lnx_bias[l][None], w_out_b[l])
        xs = _ffn(xs, mt, gains[2], up_b[l, 1], down_b[l, 1], 2)
    return xs[:, ctx_len:]
```

```python
import functools

import jax
import jax.numpy as jnp
import numpy as np
from jax import lax
from jax.experimental import pallas as pl
from jax.experimental.pallas import tpu as pltpu

F32 = jnp.float32
BF16 = jnp.bfloat16
HIGHEST = lax.Precision.HIGHEST

GRID_W = 64
NA_ROWS = 8
NA_COLS = 16
HEAD_DIM = 64
LANES = 128
TM = 256
CHUNK = 64
Q_ROWS = 4
KEY_ROWS = Q_ROWS + NA_ROWS - 1
RMS_EPS = 1e-6
LNX_EPS = 64e-5
L2_EPS = 1e-24
NEG_INF = -1e30
ATTN_SCALE = HEAD_DIM ** -0.5
VMEM_LIMIT = 56 * 1024 * 1024


def _params(*sem):
    return pltpu.CompilerParams(dimension_semantics=sem, vmem_limit_bytes=VMEM_LIMIT)


def _bdot(a, b):
    return jnp.dot(a.astype(BF16), b.astype(BF16), preferred_element_type=F32)


_NN = (((1,), (0,)), ((), ()))
_NT = (((1,), (1,)), ((), ()))
_TN = (((0,), (0,)), ((), ()))


def _mm(a, b, dims=_NN):
    return lax.dot_general(a, b, dims, preferred_element_type=F32)


def _split(a):
    hi = a.astype(BF16)
    return hi, (a - hi.astype(F32)).astype(BF16)


def _mm3(a2, b2):
    (ah, al), (bh, bl) = a2, b2
    return (_mm(ah, bl) + _mm(al, bh)) + _mm(ah, bh)


def _sigmoid(x):
    return 1.0 / (1.0 + jnp.exp(-x))


def _head0_mask(shape):
    return lax.broadcasted_iota(jnp.int32, shape, len(shape) - 1) < HEAD_DIM


def _head_sum(z):
    m0 = _head0_mask(z.shape)
    s0 = jnp.sum(jnp.where(m0, z, 0.0), axis=-1, keepdims=True)
    s1 = jnp.sum(jnp.where(m0, 0.0, z), axis=-1, keepdims=True)
    return jnp.where(m0, s0, s1)


def _norm_modulate(x, gain, shift, scale):
    y = x * lax.rsqrt(jnp.mean(x * x, axis=-1, keepdims=True) + RMS_EPS)
    return (y * gain) * (1.0 + scale) + shift


def _mod_kernel(c_ref, w_ref, b_ref, o_ref):
    c = c_ref[...]
    o_ref[0] = _bdot(c * _sigmoid(c), w_ref[0]) + b_ref[0]


def _modulation(cvec, w_mod, b_mod):
    n_layer, d, nd = w_mod.shape
    return pl.pallas_call(
        _mod_kernel,
        grid=(n_layer, nd // d),
        in_specs=[pl.BlockSpec((8, d), lambda l, j: (0, 0)),
                  pl.BlockSpec((1, d, d), lambda l, j: (l, 0, j)),
                  pl.BlockSpec((1, 1, d), lambda l, j: (l, 0, j))],
        out_specs=pl.BlockSpec((1, 8, d), lambda l, j: (l, 0, j)),
        out_shape=jax.ShapeDtypeStruct((n_layer, 8, nd), F32),
        compiler_params=_params("parallel", "parallel"),
    )(cvec, w_mod, b_mod.reshape(n_layer, 1, nd))


def _ffn_apply(x, mod, sub, gain, wup_ref, wdown_ref, n_chunk):
    d_ff = wdown_ref.shape[0]
    ck = d_ff // n_chunk
    xn = _norm_modulate(x, gain, mod[3 * sub:3 * sub + 1], mod[3 * sub + 1:3 * sub + 2]).astype(BF16)
    acc = jnp.zeros(x.shape, F32)
    for c in range(n_chunk):
        gate = jnp.dot(xn, wup_ref[:, c * ck:(c + 1) * ck], preferred_element_type=F32)
        up = jnp.dot(xn, wup_ref[:, d_ff + c * ck:d_ff + (c + 1) * ck], preferred_element_type=F32)
        h = (gate * _sigmoid(gate) * up).astype(BF16)
        acc = acc + jnp.dot(h, wdown_ref[c * ck:(c + 1) * ck, :], preferred_element_type=F32)
    return x + (0.5 * mod[3 * sub + 2:3 * sub + 3]) * acc


def _ffn_kernel(x_ref, mod_ref, gain_ref, wup_ref, wdown_ref, o_ref, *, sub, n_chunk):
    o_ref[0] = _ffn_apply(x_ref[0], mod_ref[0, 0], sub, gain_ref[...], wup_ref, wdown_ref, n_chunk)


def _row_specs(d):
    return pl.BlockSpec((1, TM, d), lambda b, t: (b, t, 0))


def _mod_spec(d):
    return pl.BlockSpec((1, 1, 9, d), lambda b, t: (b, jnp.minimum(t, 1), 0, 0))


def _const_spec(shape):
    return pl.BlockSpec(shape, lambda b, t: (0,) * len(shape), pipeline_mode=pl.Buffered(1))


def _ffn(x, modtab, gain, wup, wdown, sub):
    nb, tt, d = x.shape
    d_ff = wdown.shape[0]
    return pl.pallas_call(
        functools.partial(_ffn_kernel, sub=sub, n_chunk=2),
        grid=(nb, tt // TM),
        in_specs=[_row_specs(d), _mod_spec(d), _const_spec((1, d)),
                  _const_spec((d, 2 * d_ff)), _const_spec((d_ff, d))],
        out_specs=_row_specs(d),
        out_shape=jax.ShapeDtypeStruct(x.shape, F32),
        compiler_params=_params("parallel", "parallel"),
    )(x, modtab, gain, wup, wdown)


def _inproj_kernel(x_ref, mod_ref, gain_ref, w_ref, qg_ref, kg_ref,
                   q_ref, k_ref, v_ref, pr_ref, *, attn_w):
    mod = mod_ref[0, 0]
    xn = _norm_modulate(x_ref[0], gain_ref[...], mod[3:4], mod[4:5]).astype(BF16)
    p = jnp.dot(xn, w_ref[...], preferred_element_type=F32)
    for col in range(attn_w // LANES):
        sl = slice(col * LANES, (col + 1) * LANES)
        qc = p[:, sl]
        kc = p[:, attn_w + col * LANES:attn_w + (col + 1) * LANES]
        qn = qc * lax.rsqrt(_head_sum(qc * qc) * (1.0 / HEAD_DIM) + RMS_EPS) * qg_ref[...]
        kn = kc * lax.rsqrt(_head_sum(kc * kc) * (1.0 / HEAD_DIM) + RMS_EPS) * kg_ref[...]
        q_ref[0, :, sl] = (qn * ATTN_SCALE).astype(BF16)
        k_ref[0, :, sl] = kn.astype(BF16)
    v_ref[0] = p[:, 2 * attn_w:3 * attn_w].astype(BF16)
    pr_ref[0] = p[:, 3 * attn_w:]


def _inproj(x, modtab, gain, w_in, q_gain2, k_gain2, attn_w):
    nb, tt, d = x.shape
    d_in = w_in.shape[1]
    rw = d_in - 3 * attn_w
    return pl.pallas_call(
        functools.partial(_inproj_kernel, attn_w=attn_w),
        grid=(nb, tt // TM),
        in_specs=[_row_specs(d), _mod_spec(d), _const_spec((1, d)), _const_spec((d, d_in)),
                  _const_spec((1, LANES)), _const_spec((1, LANES))],
        out_specs=[_row_specs(attn_w), _row_specs(attn_w), _row_specs(attn_w), _row_specs(rw)],
        out_shape=[jax.ShapeDtypeStruct((nb, tt, attn_w), BF16)] * 3
        + [jax.ShapeDtypeStruct((nb, tt, rw), F32)],
        compiler_params=_params("parallel", "parallel"),
    )(x, modtab, gain, w_in, q_gain2, k_gain2)


_TAB_PAD = LANES
_TAB_W = 11 * LANES


def _bias_layout(n_blk):
    rows = n_blk * Q_ROWS
    pats = []
    for blk in range(n_blk):
        key_row0 = int(np.clip(Q_ROWS * blk - NA_ROWS // 2, 0, rows - KEY_ROWS))
        pat = []
        for m in range(Q_ROWS):
            qi = Q_ROWS * blk + m
            start = int(np.clip(qi - NA_ROWS // 2, 0, rows - NA_ROWS))
            dst = (start - key_row0) * GRID_W
            off = _TAB_PAD + (start - qi + NA_ROWS - 1) * GRID_W - dst
            copy = (off % LANES) // HEAD_DIM
            base = off + copy * HEAD_DIM
            assert base >= 0 and base % LANES == 0 and base + KEY_ROWS * GRID_W <= _TAB_W
            pat.append((copy, base, dst))
        pats.append(tuple(pat))
    assert all(p == pats[1] for p in pats[1:n_blk - 1])
    return pats[0], pats[1], pats[n_blk - 1]


def _attn_kernel(q_ref, k_ref, v_ref, tab_ref, o_ref, bias_ref, *, ctx_len, n_blk):
    j = pl.program_id(2)
    q = q_ref[0]
    m0 = _head0_mask(q.shape)
    zero = jnp.zeros_like(q)
    k_ctx = k_ref[0, 0:ctx_len, :]
    v_ctx = v_ref[0, 0:ctx_len, :]
    nt = (((1,), (1,)), ((), ()))
    key_w = KEY_ROWS * GRID_W

    def head_q(hh):
        return jnp.where(m0, q, zero) if hh == 0 else jnp.where(m0, zero, q)

    lane = lax.broadcasted_iota(jnp.int32, (GRID_W, key_w), 1)
    for first_j, pat in zip((1, 2, n_blk), _bias_layout(n_blk)):
        @pl.when(j == first_j)
        def _(pat=pat):
            for hh in range(2):
                for m, (copy, base, dst) in enumerate(pat):
                    strip = tab_ref[hh, copy, :, base:base + key_w]
                    ok = jnp.logical_and(lane >= dst, lane < dst + NA_ROWS * GRID_W)
                    bias_ref[hh, m * GRID_W:(m + 1) * GRID_W, :] = jnp.where(ok, strip, NEG_INF)

    @pl.when(j == 0)
    def _():
        outs = []
        for hh in range(2):
            s = lax.dot_general(head_q(hh), k_ctx, nt, preferred_element_type=F32)
            p = jnp.exp(s - jnp.max(s, axis=-1, keepdims=True))
            o = jnp.dot(p.astype(BF16), v_ctx, preferred_element_type=F32)
            outs.append(o * (1.0 / jnp.sum(p, axis=-1, keepdims=True)))
        o_ref[0] = jnp.where(m0, outs[0], outs[1]).astype(BF16)

    @pl.when(j > 0)
    def _():
        key_row0 = jnp.clip(Q_ROWS * (j - 1) - NA_ROWS // 2, 0, n_blk * Q_ROWS - KEY_ROWS)
        row0 = pl.multiple_of(ctx_len + key_row0 * GRID_W, GRID_W)
        k_loc = k_ref[0, pl.ds(row0, key_w), :]
        v_loc = v_ref[0, pl.ds(row0, key_w), :]
        outs = []
        for hh in range(2):
            qh = head_q(hh)
            s_loc = lax.dot_general(qh, k_loc, nt, preferred_element_type=F32) + bias_ref[hh]
            s_ctx = lax.dot_general(qh, k_ctx, nt, preferred_element_type=F32)
            mx = jnp.maximum(jnp.max(s_loc, axis=-1, keepdims=True),
                             jnp.max(s_ctx, axis=-1, keepdims=True))
            p_loc = jnp.exp(s_loc - mx)
            p_ctx = jnp.exp(s_ctx - mx)
            den = jnp.sum(p_loc, axis=-1, keepdims=True) + jnp.sum(p_ctx, axis=-1, keepdims=True)
            o = (jnp.dot(p_loc.astype(BF16), v_loc, preferred_element_type=F32)
                 + jnp.dot(p_ctx.astype(BF16), v_ctx, preferred_element_type=F32))
            outs.append(o * (1.0 / den))
        o_ref[0] = jnp.where(m0, outs[0], outs[1]).astype(BF16)


def _attention(q, k, v, table, ctx_len):
    nb, tt, w = q.shape
    n_blk = (tt - ctx_len) // (Q_ROWS * GRID_W)
    q_rows = Q_ROWS * GRID_W
    assert q_rows == ctx_len == TM and n_blk >= 3
    return pl.pallas_call(
        functools.partial(_attn_kernel, ctx_len=ctx_len, n_blk=n_blk),
        grid=(w // LANES, nb, n_blk + 1),
        in_specs=[pl.BlockSpec((1, q_rows, LANES), lambda hp, b, j: (b, j, hp)),
                  pl.BlockSpec((1, tt, LANES), lambda hp, b, j: (b, 0, hp)),
                  pl.BlockSpec((1, tt, LANES), lambda hp, b, j: (b, 0, hp)),
                  pl.BlockSpec((2, 2, GRID_W, _TAB_W), lambda hp, b, j: (hp, 0, 0, 0))],
        out_specs=pl.BlockSpec((1, q_rows, LANES), lambda hp, b, j: (b, j, hp)),
        out_shape=jax.ShapeDtypeStruct((nb, tt, w), BF16),
        scratch_shapes=[pltpu.VMEM((2, q_rows, KEY_ROWS * GRID_W), F32)],
        compiler_params=_params("parallel", "parallel", "arbitrary"),
    )(q, k, v, table)


def _bias_tables(na_bias):
    n_layer, n_head, n_dr, n_dc = na_bias.shape
    cols = np.arange(GRID_W)
    col_start = np.clip(cols - NA_COLS // 2, 0, GRID_W - NA_COLS)
    in_window = (cols[None, :] >= col_start[:, None]) & (cols[None, :] < col_start[:, None] + NA_COLS)
    dc = np.clip(cols[None, :] - cols[:, None], -(NA_COLS - 1), NA_COLS - 1) + (NA_COLS - 1)
    onehot = (dc[None] == np.arange(n_dc)[:, None, None]).astype(np.float32)
    tiles = jnp.einsum('lhrj,jqk->lhqrk', na_bias.astype(F32), onehot, precision=HIGHEST)
    tiles = jnp.where(in_window[None, None, :, None, :], tiles, NEG_INF)
    strip = tiles.reshape(n_layer, n_head, GRID_W, n_dr * GRID_W)
    copies = []
    for shift in (0, HEAD_DIM):
        left = _TAB_PAD + shift
        copies.append(jnp.pad(strip, ((0, 0), (0, 0), (0, 0), (left, _TAB_W - left - n_dr * GRID_W)),
                              constant_values=NEG_INF))
    return jnp.stack(copies, axis=2)


def _prep_kernel(p_ref, hp_ref, hn_ref, mu_ref, w0_ref, w2_ref, a0_ref, a2_ref, g2_ref,
                 kk_ref, ka_ref, u_ref,
                 r_ref, v_ref, kkn_ref, g_ref, bon_ref, lw_ref, kd_ref, bb_ref, *, hw, n_tile):
    t = pl.program_id(1)
    p = p_ref[0]
    rows = lax.broadcasted_iota(jnp.int32, (TM, 1), 0)
    prev_ok = (t >= 2).astype(F32)
    next_ok = jnp.logical_and(t >= 1, t <= n_tile - 2).astype(F32)
    prev = jnp.where(rows == 0, hp_ref[0, 7:8, :] * prev_ok, pltpu.roll(p, 1, 0))
    nxt = jnp.where(rows == TM - 1, hn_ref[0, 0:1, :] * next_ok, pltpu.roll(p, TM - 1, 0))
    p = p + mu_ref[...] * (0.5 * (prev + nxt) - p)
    r = p[:, 0:hw]
    k = p[:, hw:2 * hw]
    v = p[:, 2 * hw:3 * hw]
    lora = p[:, 3 * hw:3 * hw + LANES]
    dg = p[:, 3 * hw + LANES:3 * hw + 2 * LANES]
    g_ref[0] = _bdot(_sigmoid(dg), g2_ref[...])
    r_ref[0] = r
    v_ref[0] = v
    tl = jnp.tanh(lora)
    ksum = jnp.zeros_like(k)
    kks = []
    for col in range(hw // LANES):
        sl = slice(col * LANES, (col + 1) * LANES)
        kc = k[:, sl] * kk_ref[:, sl]
        kks.append(kc * lax.rsqrt(jnp.maximum(_head_sum(kc * kc), L2_EPS)))
    kkn = jnp.concatenate(kks, axis=1)
    kkn_ref[0] = kkn
    for d in range(2):
        z = w0_ref[d:d + 1, :] + _bdot(tl, w2_ref[d])
        lw_ref[d, 0] = -np.float32(np.exp(-0.5)) * _sigmoid(z)
        a = _sigmoid(a0_ref[d:d + 1, :] + _bdot(lora, a2_ref[d]))
        kd = k * (1.0 + (a - 1.0) * ka_ref[...])
        kd_ref[d, 0] = kd
        bb_ref[d, 0] = kkn * a
        ksum = ksum + kd
    bon = r * ksum * u_ref[...]
    for col in range(hw // LANES):
        sl = slice(col * LANES, (col + 1) * LANES)
        bon_ref[0, :, sl] = _head_sum(bon[:, sl]) * v[:, sl]


def _rwkv_prepare(pr, shift_mu, w0, w2p, a0, a2p, g2, key_k, key_a, bonus_u, hw):
    nb, tt, rw = pr.shape
    n_tile = tt // TM
    sub = TM // 8
    tok = jax.ShapeDtypeStruct((nb, tt, hw), F32)
    tok2 = jax.ShapeDtypeStruct((2, nb, tt, hw), F32)
    tok_spec = _row_specs(hw)
    tok2_spec = pl.BlockSpec((2, 1, TM, hw), lambda b, t: (0, b, t, 0))
    return pl.pallas_call(
        functools.partial(_prep_kernel, hw=hw, n_tile=n_tile),
        grid=(nb, n_tile),
        in_specs=[_row_specs(rw),
                  pl.BlockSpec((1, 8, rw), lambda b, t: (b, jnp.maximum(t * sub - 1, 0), 0)),
                  pl.BlockSpec((1, 8, rw), lambda b, t: (b, jnp.minimum((t + 1) * sub, tt // 8 - 1), 0)),
                  _const_spec((1, rw)), _const_spec((2, hw)), _const_spec((2, LANES, hw)),
                  _const_spec((2, hw)), _const_spec((2, LANES, hw)), _const_spec((LANES, hw)),
                  _const_spec((1, hw)), _const_spec((1, hw)), _const_spec((1, hw))],
        out_specs=[tok_spec] * 5 + [tok2_spec] * 3,
        out_shape=[tok] * 5 + [tok2] * 3,
        compiler_params=_params("parallel", "parallel"),
    )(pr, pr, pr, shift_mu, w0, w2p, a0, a2p, g2, key_k, key_a, bonus_u)


def _stack_heads(x):
    m0 = _head0_mask(x.shape)
    zero = jnp.zeros_like(x)
    return jnp.concatenate([jnp.where(m0, x, zero), jnp.where(m0, zero, x)], axis=0)


def _scan_kernel(rf_ref, vf_ref, kkf_ref, lwf_ref, kdf_ref, bbf_ref,
                 rb_ref, vb_ref, kkb_ref, lwb_ref, kdb_ref, bbb_ref,
                 of_ref, ob_ref, st_ref, *, hw):
    n_pair = hw // LANES
    c2 = 2 * CHUNK

    @pl.when(pl.program_id(1) == 0)
    def _():
        st_ref[...] = jnp.zeros(st_ref.shape, F32)

    ti = lax.broadcasted_iota(jnp.int32, (CHUNK, CHUNK), 0)
    tj = lax.broadcasted_iota(jnp.int32, (CHUNK, CHUNK), 1)
    ri = lax.broadcasted_iota(jnp.int32, (c2, c2), 0)
    rj = lax.broadcasted_iota(jnp.int32, (c2, c2), 1)
    same = (ri // CHUNK) == (rj // CHUNK)
    blk_r, blk_c = ri % CHUNK, rj % CHUNK
    eye = (ri == rj).astype(F32)
    li = lax.broadcasted_iota(jnp.int32, (LANES, LANES), 0)
    lj = lax.broadcasted_iota(jnp.int32, (LANES, LANES), 1)
    eye_l = (li == lj).astype(F32)
    sls = [slice(pr * LANES, (pr + 1) * LANES) for pr in range(n_pair)]

    ops = []
    for d, (r_ref, v_ref, kk_ref, lw_ref, kd_ref, bb_ref) in enumerate(
            ((rf_ref, vf_ref, kkf_ref, lwf_ref, kdf_ref, bbf_ref),
             (rb_ref, vb_ref, kkb_ref, lwb_ref, kdb_ref, bbb_ref))):
        incl1 = (tj <= ti) if d == 0 else (tj >= ti)
        before = (blk_c - blk_r) if d == 0 else (blk_r - blk_c)
        incl = jnp.logical_and(same, before <= 0).astype(F32)
        strict = jnp.logical_and(same, before < 0).astype(F32)
        lw = lw_ref[0, 0]
        incl1 = incl1.astype(F32).astype(BF16)
        lw1 = lw.astype(BF16)
        lw2, lw3 = _split(lw - lw1.astype(F32))
        cum = (_mm(incl1, lw3) + _mm(incl1, lw2)) + _mm(incl1, lw1)
        cum_prev = cum - lw
        tot = jnp.sum(lw, axis=0, keepdims=True)
        mid = 0.5 * tot
        r, v, kk = r_ref[0], v_ref[0], kk_ref[0]
        kd, bb = kd_ref[0, 0], bb_ref[0, 0]
        e_inv = jnp.exp(mid - cum)
        e_end = jnp.exp(tot - cum)
        full = dict(a_hat=kk * jnp.exp(cum_prev - mid), r_hat=r * jnp.exp(cum - mid),
                    k_til=kd * e_inv, b_til=bb * e_inv, v=v,
                    a_abs=kk * jnp.exp(cum_prev), r_abs=r * jnp.exp(cum),
                    k_end=kd * e_end, b_end=bb * e_end)
        e_tot = jnp.exp(tot)
        for p in range(n_pair):
            op = {name: _stack_heads(x[:, sls[p]]) for name, x in full.items()}
            op.update(incl=incl, strict=strict, e_tot=e_tot[:, sls[p]], d=d, p=p)
            ops.append(op)

    ch = range(len(ops))
    bf = lambda name: [ops[i][name].astype(BF16) for i in ch]
    a_s, r_s, k_s, b_s, v_s = bf("a_hat"), bf("r_hat"), bf("k_til"), bf("b_til"), bf("v")
    low = [ops[i]["strict"] * _mm(a_s[i], b_s[i], _NT) for i in ch]
    m_k = [(ops[i]["strict"] * _mm(a_s[i], k_s[i], _NT)).astype(BF16) for i in ch]
    m_rk = [(ops[i]["incl"] * _mm(r_s[i], k_s[i], _NT)).astype(BF16) for i in ch]
    m_rb = [(ops[i]["incl"] * _mm(r_s[i], b_s[i], _NT)).astype(BF16) for i in ch]
    t_inv = None
    s = 1
    while s < CHUNK:
        join = jnp.logical_and(blk_r // (2 * s) == blk_c // (2 * s), blk_r // s != blk_c // s)
        join = join.astype(F32)
        if t_inv is None:
            t_inv = [eye - join * low[i] for i in ch]
        else:
            d2 = [_split(t_inv[i]) for i in ch]
            ed = [_split(_mm3(_split(join * low[i]), d2[i])) for i in ch]
            t_inv = [t_inv[i] - _mm3(d2[i], ed[i]) for i in ch]
        s *= 2
    mkv = [_mm(m_k[i], v_s[i]) for i in ch]
    mrkv = [_mm(m_rk[i], v_s[i]) for i in ch]
    au = [_mm(t_inv[i].astype(BF16),
              jnp.concatenate([ops[i]["a_abs"], mkv[i]], axis=1).astype(BF16)).astype(BF16)
          for i in ch]
    qo = [jnp.concatenate([ops[i]["r_abs"], mrkv[i]], axis=1) - _mm(m_rb[i], au[i]) for i in ch]
    ba = [_mm(ops[i]["b_end"].astype(BF16), au[i], _TN) for i in ch]
    kv = [_mm(ops[i]["k_end"].astype(BF16), v_s[i], _TN) for i in ch]
    st = [_split(st_ref[ops[i]["d"], ops[i]["p"]]) for i in ch]
    for i in ch:
        o_s = _mm(qo[i][:, :LANES].astype(BF16), st[i][0]) + qo[i][:, LANES:]
        o_ref = of_ref if ops[i]["d"] == 0 else ob_ref
        o_ref[0, :, sls[ops[i]["p"]]] = o_s[:CHUNK] + o_s[CHUNK:]
    for i in ch:
        g_mat = eye_l * ops[i]["e_tot"] - ba[i][:, :LANES]
        st_ref[ops[i]["d"], ops[i]["p"]] = _mm3(_split(g_mat), st[i]) + (kv[i] - ba[i][:, LANES:])


def _rwkv_scan(r, v, kk, lw, kd, bb, ctx_len):
    nb, tt, hw = r.shape
    n_chunk = tt // CHUNK
    n_ctx = ctx_len // CHUNK

    def back(c):
        return jnp.where(c < n_ctx, n_ctx - 1 - c, n_chunk + n_ctx - 1 - c)

    fwd = pl.BlockSpec((1, CHUNK, hw), lambda b, c: (b, c, 0))
    bwd = pl.BlockSpec((1, CHUNK, hw), lambda b, c: (b, back(c), 0))
    fwd_d = pl.BlockSpec((1, 1, CHUNK, hw), lambda b, c: (0, b, c, 0))
    bwd_d = pl.BlockSpec((1, 1, CHUNK, hw), lambda b, c: (1, b, back(c), 0))
    return pl.pallas_call(
        functools.partial(_scan_kernel, hw=hw),
        grid=(nb, n_chunk),
        in_specs=[fwd, fwd, fwd, fwd_d, fwd_d, fwd_d, bwd, bwd, bwd, bwd_d, bwd_d, bwd_d],
        out_specs=[fwd, bwd],
        out_shape=[jax.ShapeDtypeStruct((nb, tt, hw), F32)] * 2,
        scratch_shapes=[pltpu.VMEM((2, hw // LANES, LANES, LANES), F32)],
        compiler_params=_params("parallel", "arbitrary"),
    )(r, v, kk, lw, kd, bb, r, v, kk, lw, kd, bb)


def _outproj_kernel(x_ref, mod_ref, attn_ref, of_ref, ob_ref, bon_ref, g_ref, lg_ref, lb_ref,
                    w_ref, o_ref, *, hw):
    o = of_ref[0] + ob_ref[0]
    cols = []
    for col in range(hw // LANES):
        sl = slice(col * LANES, (col + 1) * LANES)
        oc = o[:, sl]
        dev = oc - _head_sum(oc) * (1.0 / HEAD_DIM)
        var = _head_sum(dev * dev) * (1.0 / HEAD_DIM)
        cols.append(dev * lax.rsqrt(var + LNX_EPS))
    on = jnp.concatenate(cols, axis=1) * lg_ref[...] + lb_ref[...]
    rw = ((on + bon_ref[0]) * g_ref[0]).astype(BF16)
    aw = attn_ref.shape[2]
    y = (jnp.dot(attn_ref[0], w_ref[0:aw, :], preferred_element_type=F32)
         + jnp.dot(rw, w_ref[aw:, :], preferred_element_type=F32))
    o_ref[0] = x_ref[0] + mod_ref[0, 0][5:6] * y


def _outproj(x, modtab, attn, o_fwd, o_bwd, bon, g, lnx_gain, lnx_bias, w_out):
    nb, tt, d = x.shape
    hw = bon.shape[2]
    aw = attn.shape[2]
    return pl.pallas_call(
        functools.partial(_outproj_kernel, hw=hw),
        grid=(nb, tt // TM),
        in_specs=[_row_specs(d), _mod_spec(d), _row_specs(aw), _row_specs(hw), _row_specs(hw),
                  _row_specs(hw), _row_specs(hw), _const_spec((1, hw)), _const_spec((1, hw)),
                  _const_spec((d, d))],
        out_specs=_row_specs(d),
        out_shape=jax.ShapeDtypeStruct(x.shape, F32),
        compiler_params=_params("parallel", "parallel"),
    )(x, modtab, attn, o_fwd, o_bwd, bon, g, lnx_gain, lnx_bias, w_out)


def kernel(x, c, ctx, c_ctx, w_mod, b_mod, norm_gain, ffn_up, ffn_down, w_in, q_gain, k_gain,
           na_bias, shift_mu, decay_w0, decay_w2, iclr_a0, iclr_a2, gate_g2, key_k, key_a,
           bonus_u, lnx_gain, lnx_bias, w_out):
    nb, seq, d = x.shape
    ctx_len = ctx.shape[1]
    depth = w_mod.shape[0]
    n_heads = na_bias.shape[1]
    attn_w = n_heads * HEAD_DIM
    hw = decay_w0.shape[2]
    lora_w = decay_w2.shape[2]
    assert ctx_len == TM and seq % (Q_ROWS * GRID_W) == 0 and attn_w % LANES == 0 and hw % LANES == 0
    assert lora_w + iclr_a2.shape[2] == LANES and gate_g2.shape[1] == LANES and nb + 1 <= 8

    cvec = jnp.zeros((8, d), F32).at[:nb].set(c).at[nb].set(c_ctx)
    mods = _modulation(cvec, w_mod, b_mod).reshape(depth, 8, 9, d)
    modtab = jnp.stack([jnp.broadcast_to(mods[:, nb:nb + 1], (depth, nb, 9, d)), mods[:, :nb]], axis=2)

    up_b, down_b = ffn_up.astype(BF16), ffn_down.astype(BF16)
    w_in_b, w_out_b = w_in.astype(BF16), w_out.astype(BF16)
    zpad = jnp.zeros_like(decay_w2)
    w2p = jnp.concatenate([decay_w2, zpad], axis=2)
    a2p = jnp.concatenate([zpad, iclr_a2], axis=2)
    bias_tabs = _bias_tables(na_bias)

    xs = jnp.concatenate([ctx, x], axis=1)
    for l in range(depth):
        mt = modtab[l]
        gains = norm_gain[l].reshape(3, 1, d)
        xs = _ffn(xs, mt, gains[0], up_b[l, 0], down_b[l, 0], 0)
        q, k, v, pr = _inproj(xs, mt, gains[1], w_in_b[l],
                              jnp.tile(q_gain[l], 2)[None], jnp.tile(k_gain[l], 2)[None], attn_w)
        attn = _attention(q, k, v, bias_tabs[l], ctx_len)
        r, vr, kk, g, bon, lw, kd, bb = _rwkv_prepare(
            pr, shift_mu[l][None], decay_w0[l], w2p[l], iclr_a0[l], a2p[l], gate_g2[l],
            key_k[l][None], key_a[l][None], bonus_u[l].reshape(1, hw), hw)
        o_fwd, o_bwd = _rwkv_scan(r, vr, kk, lw, kd, bb, ctx_len)
        xs = _outproj(xs, mt, attn, o_fwd, o_bwd, bon, g, lnx_gain[l][None], lnx_bias[l][None],
                      w_out_b[l])
        xs = _ffn(xs, mt, gains[2], up_b[l, 1], down_b[l, 1], 2)
    return xs[:, ctx_len:]
```

```python
import functools

import jax
import jax.numpy as jnp
import numpy as np
from jax import lax
from jax.experimental import pallas as pl
from jax.experimental.pallas import tpu as pltpu

F32 = jnp.float32
BF16 = jnp.bfloat16
HIGHEST = lax.Precision.HIGHEST

GRID_W = 64
NA_ROWS = 8
NA_COLS = 16
HEAD_DIM = 64
LANES = 128
TM = 256
CHUNK = 64
Q_ROWS = 4
KEY_ROWS = Q_ROWS + NA_ROWS - 1
RMS_EPS = 1e-6
LNX_EPS = 64e-5
L2_EPS = 1e-24
NEG_INF = -1e30
ATTN_SCALE = HEAD_DIM ** -0.5
VMEM_LIMIT = 56 * 1024 * 1024


def _params(*sem):
    return pltpu.CompilerParams(dimension_semantics=sem, vmem_limit_bytes=VMEM_LIMIT)


def _bdot(a, b):
    return jnp.dot(a.astype(BF16), b.astype(BF16), preferred_element_type=F32)


_NN = (((1,), (0,)), ((), ()))
_NT = (((1,), (1,)), ((), ()))
_TN = (((0,), (0,)), ((), ()))


def _mm(a, b, dims=_NN):
    return lax.dot_general(a, b, dims, preferred_element_type=F32)


def _split(a):
    hi = a.astype(BF16)
    return hi, (a - hi.astype(F32)).astype(BF16)


def _mm3(a2, b2):
    (ah, al), (bh, bl) = a2, b2
    return (_mm(ah, bl) + _mm(al, bh)) + _mm(ah, bh)


def _sigmoid(x):
    return 1.0 / (1.0 + jnp.exp(-x))


def _head0_mask(shape):
    return lax.broadcasted_iota(jnp.int32, shape, len(shape) - 1) < HEAD_DIM


def _head_sum(z):
    m0 = _head0_mask(z.shape)
    s0 = jnp.sum(jnp.where(m0, z, 0.0), axis=-1, keepdims=True)
    s1 = jnp.sum(jnp.where(m0, 0.0, z), axis=-1, keepdims=True)
    return jnp.where(m0, s0, s1)


def _norm_modulate(x, gain, shift, scale):
    y = x * lax.rsqrt(jnp.mean(x * x, axis=-1, keepdims=True) + RMS_EPS)
    return (y * gain) * (1.0 + scale) + shift


def _mod_kernel(c_ref, w_ref, b_ref, o_ref):
    c = c_ref[...]
    o_ref[0] = _bdot(c * _sigmoid(c), w_ref[0]) + b_ref[0]


def _modulation(cvec, w_mod, b_mod):
    n_layer, d, nd = w_mod.shape
    return pl.pallas_call(
        _mod_kernel,
        grid=(n_layer, nd // d),
        in_specs=[pl.BlockSpec((8, d), lambda l, j: (0, 0)),
                  pl.BlockSpec((1, d, d), lambda l, j: (l, 0, j)),
                  pl.BlockSpec((1, 1, d), lambda l, j: (l, 0, j))],
        out_specs=pl.BlockSpec((1, 8, d), lambda l, j: (l, 0, j)),
        out_shape=jax.ShapeDtypeStruct((n_layer, 8, nd), F32),
        compiler_params=_params("parallel", "parallel"),
    )(cvec, w_mod, b_mod.reshape(n_layer, 1, nd))


def _ffn_apply(x, mod, sub, gain, wup_ref, wdown_ref, n_chunk):
    d_ff = wdown_ref.shape[0]
    ck = d_ff // n_chunk
    xn = _norm_modulate(x, gain, mod[3 * sub:3 * sub + 1], mod[3 * sub + 1:3 * sub + 2]).astype(BF16)
    acc = jnp.zeros(x.shape, F32)
    for c in range(n_chunk):
        gate = jnp.dot(xn, wup_ref[:, c * ck:(c + 1) * ck], preferred_element_type=F32)
        up = jnp.dot(xn, wup_ref[:, d_ff + c * ck:d_ff + (c + 1) * ck], preferred_element_type=F32)
        h = (gate * _sigmoid(gate) * up).astype(BF16)
        acc = acc + jnp.dot(h, wdown_ref[c * ck:(c + 1) * ck, :], preferred_element_type=F32)
    return x + (0.5 * mod[3 * sub + 2:3 * sub + 3]) * acc


FFN_CHUNKS = 2


def _row_specs(d, first=0):
    return pl.BlockSpec((1, TM, d), lambda b, t: (b, t + first, 0))


def _mod_spec(d, first=0):
    return pl.BlockSpec((1, 1, 9, d), lambda b, t: (b, jnp.minimum(t + first, 1), 0, 0))


def _const_spec(shape):
    return pl.BlockSpec(shape, lambda b, t: (0,) * len(shape), pipeline_mode=pl.Buffered(1))


def _front_kernel(x_ref, mod_ref, gain_ref, wup_ref, wdown_ref, w_ref, qg_ref, kg_ref,
                  xo_ref, q_ref, k_ref, v_ref, pr_ref, *, attn_w):
    mod = mod_ref[0, 0]
    x = _ffn_apply(x_ref[0], mod, 0, gain_ref[0:1], wup_ref, wdown_ref, FFN_CHUNKS)
    xo_ref[0] = x
    xn = _norm_modulate(x, gain_ref[1:2], mod[3:4], mod[4:5]).astype(BF16)
    p = jnp.dot(xn, w_ref[...], preferred_element_type=F32)
    for col in range(attn_w // LANES):
        sl = slice(col * LANES, (col + 1) * LANES)
        qc = p[:, sl]
        kc = p[:, attn_w + col * LANES:attn_w + (col + 1) * LANES]
        qn = qc * lax.rsqrt(_head_sum(qc * qc) * (1.0 / HEAD_DIM) + RMS_EPS) * qg_ref[...]
        kn = kc * lax.rsqrt(_head_sum(kc * kc) * (1.0 / HEAD_DIM) + RMS_EPS) * kg_ref[...]
        q_ref[0, :, sl] = (qn * ATTN_SCALE).astype(BF16)
        k_ref[0, :, sl] = kn.astype(BF16)
    v_ref[0] = p[:, 2 * attn_w:3 * attn_w].astype(BF16)
    pr_ref[0] = p[:, 3 * attn_w:]


def _front(x, modtab, gains, wup, wdown, w_in, q_gain2, k_gain2, attn_w):
    nb, tt, d = x.shape
    d_in = w_in.shape[1]
    d_ff = wdown.shape[0]
    rw = d_in - 3 * attn_w
    return pl.pallas_call(
        functools.partial(_front_kernel, attn_w=attn_w),
        grid=(nb, tt // TM),
        in_specs=[_row_specs(d), _mod_spec(d), _const_spec((3, d)),
                  _const_spec((d, 2 * d_ff)), _const_spec((d_ff, d)), _const_spec((d, d_in)),
                  _const_spec((1, LANES)), _const_spec((1, LANES))],
        out_specs=[_row_specs(d), _row_specs(attn_w), _row_specs(attn_w), _row_specs(attn_w),
                   _row_specs(rw)],
        out_shape=[jax.ShapeDtypeStruct(x.shape, F32)]
        + [jax.ShapeDtypeStruct((nb, tt, attn_w), BF16)] * 3
        + [jax.ShapeDtypeStruct((nb, tt, rw), F32)],
        compiler_params=_params("parallel", "parallel"),
    )(x, modtab, gains, wup, wdown, w_in, q_gain2, k_gain2)


_TAB_PAD = LANES
_TAB_W = 11 * LANES


def _bias_layout(n_blk):
    rows = n_blk * Q_ROWS
    pats = []
    for blk in range(n_blk):
        key_row0 = int(np.clip(Q_ROWS * blk - NA_ROWS // 2, 0, rows - KEY_ROWS))
        pat = []
        for m in range(Q_ROWS):
            qi = Q_ROWS * blk + m
            start = int(np.clip(qi - NA_ROWS // 2, 0, rows - NA_ROWS))
            dst = (start - key_row0) * GRID_W
            off = _TAB_PAD + (start - qi + NA_ROWS - 1) * GRID_W - dst
            copy = (off % LANES) // HEAD_DIM
            base = off + copy * HEAD_DIM
            assert base >= 0 and base % LANES == 0 and base + KEY_ROWS * GRID_W <= _TAB_W
            pat.append((copy, base, dst))
        pats.append(tuple(pat))
    assert all(p == pats[1] for p in pats[1:n_blk - 1])
    return pats[0], pats[1], pats[n_blk - 1]


def _attn_kernel(q_ref, k_ref, v_ref, tab_ref, o_ref, bias_ref, *, ctx_len, n_blk):
    j = pl.program_id(2)
    q = q_ref[0]
    m0 = _head0_mask(q.shape)
    zero = jnp.zeros_like(q)
    k_ctx = k_ref[0, 0:ctx_len, :]
    v_ctx = v_ref[0, 0:ctx_len, :]
    nt = (((1,), (1,)), ((), ()))
    key_w = KEY_ROWS * GRID_W

    def head_q(hh):
        return jnp.where(m0, q, zero) if hh == 0 else jnp.where(m0, zero, q)

    lane = lax.broadcasted_iota(jnp.int32, (GRID_W, key_w), 1)
    for first_j, pat in zip((1, 2, n_blk), _bias_layout(n_blk)):
        @pl.when(j == first_j)
        def _(pat=pat):
            for hh in range(2):
                for m, (copy, base, dst) in enumerate(pat):
                    strip = tab_ref[hh, copy, :, base:base + key_w]
                    ok = jnp.logical_and(lane >= dst, lane < dst + NA_ROWS * GRID_W)
                    bias_ref[hh, m * GRID_W:(m + 1) * GRID_W, :] = jnp.where(ok, strip, NEG_INF)

    q2 = jnp.concatenate([head_q(0), head_q(1)], axis=0)
    n_q = q.shape[0]

    @pl.when(j == 0)
    def _():
        s = lax.dot_general(q2, k_ctx, nt, preferred_element_type=F32)
        p = jnp.exp(s - jnp.max(s, axis=-1, keepdims=True))
        o = jnp.dot(p.astype(BF16), v_ctx, preferred_element_type=F32)
        o = o * (1.0 / jnp.sum(p, axis=-1, keepdims=True))
        o_ref[0] = jnp.where(m0, o[:n_q], o[n_q:]).astype(BF16)

    @pl.when(j > 0)
    def _():
        key_row0 = jnp.clip(Q_ROWS * (j - 1) - NA_ROWS // 2, 0, n_blk * Q_ROWS - KEY_ROWS)
        row0 = pl.multiple_of(ctx_len + key_row0 * GRID_W, GRID_W)
        k_loc = k_ref[0, pl.ds(row0, key_w), :]
        v_loc = v_ref[0, pl.ds(row0, key_w), :]
        s_loc = (lax.dot_general(q2, k_loc, nt, preferred_element_type=F32)
                 + bias_ref[...].reshape(2 * n_q, key_w))
        s_ctx = lax.dot_general(q2, k_ctx, nt, preferred_element_type=F32)
        mx = jnp.maximum(jnp.max(s_loc, axis=-1, keepdims=True),
                         jnp.max(s_ctx, axis=-1, keepdims=True))
        p_loc = jnp.exp(s_loc - mx)
        p_ctx = jnp.exp(s_ctx - mx)
        den = jnp.sum(p_loc, axis=-1, keepdims=True) + jnp.sum(p_ctx, axis=-1, keepdims=True)
        o = (jnp.dot(p_loc.astype(BF16), v_loc, preferred_element_type=F32)
             + jnp.dot(p_ctx.astype(BF16), v_ctx, preferred_element_type=F32))
        o = o * (1.0 / den)
        o_ref[0] = jnp.where(m0, o[:n_q], o[n_q:]).astype(BF16)


def _attention(q, k, v, table, ctx_len):
    nb, tt, w = q.shape
    n_blk = (tt - ctx_len) // (Q_ROWS * GRID_W)
    q_rows = Q_ROWS * GRID_W
    assert q_rows == ctx_len == TM and n_blk >= 3
    return pl.pallas_call(
        functools.partial(_attn_kernel, ctx_len=ctx_len, n_blk=n_blk),
        grid=(w // LANES, nb, n_blk + 1),
        in_specs=[pl.BlockSpec((1, q_rows, LANES), lambda hp, b, j: (b, j, hp)),
                  pl.BlockSpec((1, tt, LANES), lambda hp, b, j: (b, 0, hp)),
                  pl.BlockSpec((1, tt, LANES), lambda hp, b, j: (b, 0, hp)),
                  pl.BlockSpec((2, 2, GRID_W, _TAB_W), lambda hp, b, j: (hp, 0, 0, 0))],
        out_specs=pl.BlockSpec((1, q_rows, LANES), lambda hp, b, j: (b, j, hp)),
        out_shape=jax.ShapeDtypeStruct((nb, tt, w), BF16),
        scratch_shapes=[pltpu.VMEM((2, q_rows, KEY_ROWS * GRID_W), F32)],
        compiler_params=_params("parallel", "parallel", "arbitrary"),
    )(q, k, v, table)


def _bias_tables(na_bias):
    n_layer, n_head, n_dr, n_dc = na_bias.shape
    cols = np.arange(GRID_W)
    col_start = np.clip(cols - NA_COLS // 2, 0, GRID_W - NA_COLS)
    in_window = (cols[None, :] >= col_start[:, None]) & (cols[None, :] < col_start[:, None] + NA_COLS)
    dc = np.clip(cols[None, :] - cols[:, None], -(NA_COLS - 1), NA_COLS - 1) + (NA_COLS - 1)
    onehot = (dc[None] == np.arange(n_dc)[:, None, None]).astype(np.float32)
    tiles = jnp.einsum('lhrj,jqk->lhqrk', na_bias.astype(F32), onehot, precision=HIGHEST)
    tiles = jnp.where(in_window[None, None, :, None, :], tiles, NEG_INF)
    strip = tiles.reshape(n_layer, n_head, GRID_W, n_dr * GRID_W)
    copies = []
    for shift in (0, HEAD_DIM):
        left = _TAB_PAD + shift
        copies.append(jnp.pad(strip, ((0, 0), (0, 0), (0, 0), (left, _TAB_W - left - n_dr * GRID_W)),
                              constant_values=NEG_INF))
    return jnp.stack(copies, axis=2)


def _prep_kernel(p_ref, hp_ref, hn_ref, mu_ref, w0_ref, w2_ref, a0_ref, a2_ref, g2_ref,
                 kk_ref, ka_ref, u_ref,
                 r_ref, v_ref, kkn_ref, g_ref, bon_ref, lw_ref, kd_ref, bb_ref, *, hw, n_tile):
    t = pl.program_id(1)
    p = p_ref[0]
    rows = lax.broadcasted_iota(jnp.int32, (8, 1), 0)
    prev_ok = (t >= 2).astype(F32)
    next_ok = jnp.logical_and(t >= 1, t <= n_tile - 2).astype(F32)
    prev = pltpu.roll(p, 1, 0)
    nxt = pltpu.roll(p, TM - 1, 0)
    prev = jnp.concatenate(
        [jnp.where(rows == 0, hp_ref[0, 7:8, :] * prev_ok, prev[0:8]), prev[8:]], axis=0)
    nxt = jnp.concatenate(
        [nxt[:TM - 8], jnp.where(rows == 7, hn_ref[0, 0:1, :] * next_ok, nxt[TM - 8:])], axis=0)
    p = p + mu_ref[...] * (0.5 * (prev + nxt) - p)
    r = p[:, 0:hw]
    k = p[:, hw:2 * hw]
    v = p[:, 2 * hw:3 * hw]
    lora = p[:, 3 * hw:3 * hw + LANES]
    dg = p[:, 3 * hw + LANES:3 * hw + 2 * LANES]
    g_ref[0] = _bdot(_sigmoid(dg), g2_ref[...])
    r_ref[0] = r
    v_ref[0] = v
    tl = jnp.tanh(lora)
    ksum = jnp.zeros_like(k)
    kks = []
    for col in range(hw // LANES):
        sl = slice(col * LANES, (col + 1) * LANES)
        kc = k[:, sl] * kk_ref[:, sl]
        kks.append(kc * lax.rsqrt(jnp.maximum(_head_sum(kc * kc), L2_EPS)))
    kkn = jnp.concatenate(kks, axis=1)
    kkn_ref[0] = kkn
    for d in range(2):
        z = w0_ref[d:d + 1, :] + _bdot(tl, w2_ref[d])
        lw_ref[d, 0] = -np.float32(np.exp(-0.5)) * _sigmoid(z)
        a = _sigmoid(a0_ref[d:d + 1, :] + _bdot(lora, a2_ref[d]))
        kd = k * (1.0 + (a - 1.0) * ka_ref[...])
        kd_ref[d, 0] = kd
        bb_ref[d, 0] = kkn * a
        ksum = ksum + kd
    bon = r * ksum * u_ref[...]
    for col in range(hw // LANES):
        sl = slice(col * LANES, (col + 1) * LANES)
        bon_ref[0, :, sl] = _head_sum(bon[:, sl]) * v[:, sl]


def _rwkv_prepare(pr, shift_mu, w0, w2p, a0, a2p, g2, key_k, key_a, bonus_u, hw):
    nb, tt, rw = pr.shape
    n_tile = tt // TM
    sub = TM // 8
    tok = jax.ShapeDtypeStruct((nb, tt, hw), F32)
    tok2 = jax.ShapeDtypeStruct((2, nb, tt, hw), F32)
    tok_spec = _row_specs(hw)
    tok2_spec = pl.BlockSpec((2, 1, TM, hw), lambda b, t: (0, b, t, 0))
    return pl.pallas_call(
        functools.partial(_prep_kernel, hw=hw, n_tile=n_tile),
        grid=(nb, n_tile),
        in_specs=[_row_specs(rw),
                  pl.BlockSpec((1, 8, rw), lambda b, t: (b, jnp.maximum(t * sub - 1, 0), 0)),
                  pl.BlockSpec((1, 8, rw), lambda b, t: (b, jnp.minimum((t + 1) * sub, tt // 8 - 1), 0)),
                  _const_spec((1, rw)), _const_spec((2, hw)), _const_spec((2, LANES, hw)),
                  _const_spec((2, hw)), _const_spec((2, LANES, hw)), _const_spec((LANES, hw)),
                  _const_spec((1, hw)), _const_spec((1, hw)), _const_spec((1, hw))],
        out_specs=[tok_spec] * 5 + [tok2_spec] * 3,
        out_shape=[tok] * 5 + [tok2] * 3,
        compiler_params=_params("parallel", "parallel"),
    )(pr, pr, pr, shift_mu, w0, w2p, a0, a2p, g2, key_k, key_a, bonus_u)


def _stack_heads(x):
    m0 = _head0_mask(x.shape)
    zero = jnp.zeros_like(x)
    return jnp.concatenate([jnp.where(m0, x, zero), jnp.where(m0, zero, x)], axis=0)


def _scan_kernel(rf_ref, vf_ref, kkf_ref, lwf_ref, kdf_ref, bbf_ref,
                 rb_ref, vb_ref, kkb_ref, lwb_ref, kdb_ref, bbb_ref,
                 of_ref, ob_ref, st_ref, *, hw):
    n_pair = hw // LANES
    c2 = 2 * CHUNK

    @pl.when(pl.program_id(1) == 0)
    def _():
        st_ref[...] = jnp.zeros(st_ref.shape, F32)

    ti = lax.broadcasted_iota(jnp.int32, (CHUNK, CHUNK), 0)
    tj = lax.broadcasted_iota(jnp.int32, (CHUNK, CHUNK), 1)
    ri = lax.broadcasted_iota(jnp.int32, (c2, c2), 0)
    rj = lax.broadcasted_iota(jnp.int32, (c2, c2), 1)
    same = (ri // CHUNK) == (rj // CHUNK)
    blk_r, blk_c = ri % CHUNK, rj % CHUNK
    eye = (ri == rj).astype(F32)
    li = lax.broadcasted_iota(jnp.int32, (LANES, LANES), 0)
    lj = lax.broadcasted_iota(jnp.int32, (LANES, LANES), 1)
    eye_l = (li == lj).astype(F32)
    sls = [slice(pr * LANES, (pr + 1) * LANES) for pr in range(n_pair)]

    ops = []
    for d, (r_ref, v_ref, kk_ref, lw_ref, kd_ref, bb_ref) in enumerate(
            ((rf_ref, vf_ref, kkf_ref, lwf_ref, kdf_ref, bbf_ref),
             (rb_ref, vb_ref, kkb_ref, lwb_ref, kdb_ref, bbb_ref))):
        incl1 = (tj <= ti) if d == 0 else (tj >= ti)
        before = (blk_c - blk_r) if d == 0 else (blk_r - blk_c)
        incl = jnp.logical_and(same, before <= 0).astype(F32)
        strict = jnp.logical_and(same, before < 0).astype(F32)
        lw = lw_ref[0, 0]
        incl1 = incl1.astype(F32).astype(BF16)
        lw1 = lw.astype(BF16)
        lw2, lw3 = _split(lw - lw1.astype(F32))
        cum = (_mm(incl1, lw3) + _mm(incl1, lw2)) + _mm(incl1, lw1)
        cum_prev = cum - lw
        tot = jnp.sum(lw, axis=0, keepdims=True)
        mid = 0.5 * tot
        r, v, kk = r_ref[0], v_ref[0], kk_ref[0]
        kd, bb = kd_ref[0, 0], bb_ref[0, 0]
        e_inv = jnp.exp(mid - cum)
        e_end = jnp.exp(tot - cum)
        full = dict(a_hat=kk * jnp.exp(cum_prev - mid), r_hat=r * jnp.exp(cum - mid),
                    k_til=kd * e_inv, b_til=bb * e_inv, v=v,
                    a_abs=kk * jnp.exp(cum_prev), r_abs=r * jnp.exp(cum),
                    k_end=kd * e_end, b_end=bb * e_end)
        e_tot = jnp.exp(tot)
        for p in range(n_pair):
            op = {name: _stack_heads(x[:, sls[p]]) for name, x in full.items()}
            op.update(incl=incl, strict=strict, e_tot=e_tot[:, sls[p]], d=d, p=p)
            ops.append(op)

    ch = range(len(ops))
    bf = lambda name: [ops[i][name].astype(BF16) for i in ch]
    a_s, r_s, k_s, b_s, v_s = bf("a_hat"), bf("r_hat"), bf("k_til"), bf("b_til"), bf("v")
    ar_s = [jnp.concatenate([a_s[i], r_s[i]], axis=0) for i in ch]
    ar_b = [_mm(ar_s[i], b_s[i], _NT) for i in ch]
    ar_k = [_mm(ar_s[i], k_s[i], _NT) for i in ch]
    low = [ops[i]["strict"] * ar_b[i][:c2] for i in ch]
    m_rb = [(ops[i]["incl"] * ar_b[i][c2:]).astype(BF16) for i in ch]
    m_kk = [jnp.concatenate([ops[i]["strict"] * ar_k[i][:c2], ops[i]["incl"] * ar_k[i][c2:]],
                            axis=0).astype(BF16) for i in ch]
    t_inv = None
    s = 1
    while s < CHUNK:
        join = jnp.logical_and(blk_r // (2 * s) == blk_c // (2 * s), blk_r // s != blk_c // s)
        join = join.astype(F32)
        if t_inv is None:
            t_inv = [eye - join * low[i] for i in ch]
        else:
            d2 = [t_inv[i].astype(BF16) for i in ch]
            ed = [_mm((join * low[i]).astype(BF16), d2[i]).astype(BF16) for i in ch]
            t_inv = [t_inv[i] - _mm(d2[i], ed[i]) for i in ch]
        s *= 2
    mv = [_mm(m_kk[i], v_s[i]) for i in ch]
    au = [_mm(t_inv[i].astype(BF16),
              jnp.concatenate([ops[i]["a_abs"], mv[i][:c2]], axis=1).astype(BF16)).astype(BF16)
          for i in ch]
    qo = [jnp.concatenate([ops[i]["r_abs"], mv[i][c2:]], axis=1) - _mm(m_rb[i], au[i]) for i in ch]
    ba = [_mm(ops[i]["b_end"].astype(BF16), au[i], _TN) for i in ch]
    kv = [_mm(ops[i]["k_end"].astype(BF16), v_s[i], _TN) for i in ch]
    st = [_split(st_ref[ops[i]["d"], ops[i]["p"]]) for i in ch]
    for i in ch:
        o_s = _mm(qo[i][:, :LANES].astype(BF16), st[i][0]) + qo[i][:, LANES:]
        o_ref = of_ref if ops[i]["d"] == 0 else ob_ref
        o_ref[0, :, sls[ops[i]["p"]]] = o_s[:CHUNK] + o_s[CHUNK:]
    for i in ch:
        g_mat = eye_l * ops[i]["e_tot"] - ba[i][:, :LANES]
        st_ref[ops[i]["d"], ops[i]["p"]] = _mm3(_split(g_mat), st[i]) + (kv[i] - ba[i][:, LANES:])


def _rwkv_scan(r, v, kk, lw, kd, bb, ctx_len):
    nb, tt, hw = r.shape
    n_chunk = tt // CHUNK
    n_ctx = ctx_len // CHUNK

    def back(c):
        return jnp.where(c < n_ctx, n_ctx - 1 - c, n_chunk + n_ctx - 1 - c)

    fwd = pl.BlockSpec((1, CHUNK, hw), lambda b, c: (b, c, 0))
    bwd = pl.BlockSpec((1, CHUNK, hw), lambda b, c: (b, back(c), 0))
    fwd_d = pl.BlockSpec((1, 1, CHUNK, hw), lambda b, c: (0, b, c, 0))
    bwd_d = pl.BlockSpec((1, 1, CHUNK, hw), lambda b, c: (1, b, back(c), 0))
    return pl.pallas_call(
        functools.partial(_scan_kernel, hw=hw),
        grid=(nb, n_chunk),
        in_specs=[fwd, fwd, fwd, fwd_d, fwd_d, fwd_d, bwd, bwd, bwd, bwd_d, bwd_d, bwd_d],
        out_specs=[fwd, bwd],
        out_shape=[jax.ShapeDtypeStruct((nb, tt, hw), F32)] * 2,
        scratch_shapes=[pltpu.VMEM((2, hw // LANES, LANES, LANES), F32)],
        compiler_params=_params("parallel", "arbitrary"),
    )(r, v, kk, lw, kd, bb, r, v, kk, lw, kd, bb)


def _back_kernel(x_ref, mod_ref, gain_ref, attn_ref, of_ref, ob_ref, bon_ref, g_ref, lg_ref, lb_ref,
                 w_ref, wup_ref, wdown_ref, o_ref, *, hw):
    mod = mod_ref[0, 0]
    o = of_ref[0] + ob_ref[0]
    cols = []
    for col in range(hw // LANES):
        sl = slice(col * LANES, (col + 1) * LANES)
        oc = o[:, sl]
        dev = oc - _head_sum(oc) * (1.0 / HEAD_DIM)
        var = _head_sum(dev * dev) * (1.0 / HEAD_DIM)
        cols.append(dev * lax.rsqrt(var + LNX_EPS))
    on = jnp.concatenate(cols, axis=1) * lg_ref[...] + lb_ref[...]
    rw = ((on + bon_ref[0]) * g_ref[0]).astype(BF16)
    aw = attn_ref.shape[2]
    y = (jnp.dot(attn_ref[0], w_ref[0:aw, :], preferred_element_type=F32)
         + jnp.dot(rw, w_ref[aw:, :], preferred_element_type=F32))
    x = x_ref[0] + mod[5:6] * y
    o_ref[0] = _ffn_apply(x, mod, 2, gain_ref[2:3], wup_ref, wdown_ref, FFN_CHUNKS)


def _back(x, modtab, gains, attn, o_fwd, o_bwd, bon, g, lnx_gain, lnx_bias, w_out, wup, wdown,
          first_tile):
    nb, tt, d = x.shape
    hw = bon.shape[2]
    aw = attn.shape[2]
    d_ff = wdown.shape[0]
    n_tile = tt // TM - first_tile
    rows = lambda w: _row_specs(w, first_tile)
    return pl.pallas_call(
        functools.partial(_back_kernel, hw=hw),
        grid=(nb, n_tile),
        in_specs=[rows(d), _mod_spec(d, first_tile), _const_spec((3, d)), rows(aw), rows(hw),
                  rows(hw), rows(hw), rows(hw), _const_spec((1, hw)), _const_spec((1, hw)),
                  _const_spec((d, d)), _const_spec((d, 2 * d_ff)), _const_spec((d_ff, d))],
        out_specs=_row_specs(d),
        out_shape=jax.ShapeDtypeStruct((nb, n_tile * TM, d), F32),
        compiler_params=_params("parallel", "parallel"),
    )(x, modtab, gains, attn, o_fwd, o_bwd, bon, g, lnx_gain, lnx_bias, w_out, wup, wdown)


def kernel(x, c, ctx, c_ctx, w_mod, b_mod, norm_gain, ffn_up, ffn_down, w_in, q_gain, k_gain,
           na_bias, shift_mu, decay_w0, decay_w2, iclr_a0, iclr_a2, gate_g2, key_k, key_a,
           bonus_u, lnx_gain, lnx_bias, w_out):
    nb, seq, d = x.shape
    ctx_len = ctx.shape[1]
    depth = w_mod.shape[0]
    n_heads = na_bias.shape[1]
    attn_w = n_heads * HEAD_DIM
    hw = decay_w0.shape[2]
    lora_w = decay_w2.shape[2]
    assert ctx_len == TM and seq % (Q_ROWS * GRID_W) == 0 and attn_w % LANES == 0 and hw % LANES == 0
    assert lora_w + iclr_a2.shape[2] == LANES and gate_g2.shape[1] == LANES and nb + 1 <= 8

    cvec = jnp.zeros((8, d), F32).at[:nb].set(c).at[nb].set(c_ctx)
    mods = _modulation(cvec, w_mod, b_mod).reshape(depth, 8, 9, d)
    modtab = jnp.stack([jnp.broadcast_to(mods[:, nb:nb + 1], (depth, nb, 9, d)), mods[:, :nb]], axis=2)

    up_b, down_b = ffn_up.astype(BF16), ffn_down.astype(BF16)
    w_in_b, w_out_b = w_in.astype(BF16), w_out.astype(BF16)
    zpad = jnp.zeros_like(decay_w2)
    w2p = jnp.concatenate([decay_w2, zpad], axis=2)
    a2p = jnp.concatenate([zpad, iclr_a2], axis=2)
    bias_tabs = _bias_tables(na_bias)

    xs = jnp.concatenate([ctx, x], axis=1)
    for l in range(depth):
        mt = modtab[l]
        xs, q, k, v, pr = _front(xs, mt, norm_gain[l], up_b[l, 0], down_b[l, 0], w_in_b[l],
                                 jnp.tile(q_gain[l], 2)[None], jnp.tile(k_gain[l], 2)[None], attn_w)
        attn = _attention(q, k, v, bias_tabs[l], ctx_len)
        r, vr, kk, g, bon, lw, kd, bb = _rwkv_prepare(
            pr, shift_mu[l][None], decay_w0[l], w2p[l], iclr_a0[l], a2p[l], gate_g2[l],
            key_k[l][None], key_a[l][None], bonus_u[l].reshape(1, hw), hw)
        o_fwd, o_bwd = _rwkv_scan(r, vr, kk, lw, kd, bb, ctx_len)
        xs = _back(xs, mt, norm_gain[l], attn, o_fwd, o_bwd, bon, g, lnx_gain[l][None],
                   lnx_bias[l][None], w_out_b[l], up_b[l, 1], down_b[l, 1],
                   first_tile=ctx_len // TM if l == depth - 1 else 0)
    return xs
```

```python
import functools

import jax
import jax.numpy as jnp
import numpy as np
from jax import lax
from jax.experimental import pallas as pl
from jax.experimental.pallas import tpu as pltpu

F32 = jnp.float32
BF16 = jnp.bfloat16
HIGHEST = lax.Precision.HIGHEST

GRID_W = 64
NA_ROWS = 8
NA_COLS = 16
HEAD_DIM = 64
LANES = 128
TM = 256
CHUNK = 64
SCAN_BATCH = 2
Q_ROWS = 4
ATTN_ROW_PARTS = 1
KEY_ROWS = Q_ROWS + NA_ROWS - 1
RMS_EPS = 1e-6
LNX_EPS = 64e-5
L2_EPS = 1e-24
NEG_INF = -1e30
ATTN_SCALE = HEAD_DIM ** -0.5
VMEM_LIMIT = 56 * 1024 * 1024


def _params(*sem):
    return pltpu.CompilerParams(dimension_semantics=sem, vmem_limit_bytes=VMEM_LIMIT)


def _bdot(a, b):
    return jnp.dot(a.astype(BF16), b.astype(BF16), preferred_element_type=F32)


_NN = (((1,), (0,)), ((), ()))
_NT = (((1,), (1,)), ((), ()))
_TN = (((0,), (0,)), ((), ()))


def _mm(a, b, dims=_NN):
    return lax.dot_general(a, b, dims, preferred_element_type=F32)


def _split(a):
    hi = a.astype(BF16)
    return hi, (a - hi.astype(F32)).astype(BF16)


def _mm3(a2, b2):
    (ah, al), (bh, bl) = a2, b2
    return (_mm(ah, bl) + _mm(al, bh)) + _mm(ah, bh)


def _sigmoid(x):
    return 1.0 / (1.0 + jnp.exp(-x))


def _head0_mask(shape):
    return lax.broadcasted_iota(jnp.int32, shape, len(shape) - 1) < HEAD_DIM


def _head_sum(z):
    m0 = _head0_mask(z.shape)
    s0 = jnp.sum(jnp.where(m0, z, 0.0), axis=-1, keepdims=True)
    s1 = jnp.sum(jnp.where(m0, 0.0, z), axis=-1, keepdims=True)
    return jnp.where(m0, s0, s1)


def _norm_modulate(x, gain, shift, scale):
    y = x * lax.rsqrt(jnp.mean(x * x, axis=-1, keepdims=True) + RMS_EPS)
    return (y * gain) * (1.0 + scale) + shift


def _mod_kernel(c_ref, w_ref, b_ref, o_ref):
    c = c_ref[...]
    o_ref[0] = _bdot(c * _sigmoid(c), w_ref[0]) + b_ref[0]


def _modulation(cvec, w_mod, b_mod):
    n_layer, d, nd = w_mod.shape
    return pl.pallas_call(
        _mod_kernel,
        grid=(n_layer, nd // d),
        in_specs=[pl.BlockSpec((8, d), lambda l, j: (0, 0)),
                  pl.BlockSpec((1, d, d), lambda l, j: (l, 0, j)),
                  pl.BlockSpec((1, 1, d), lambda l, j: (l, 0, j))],
        out_specs=pl.BlockSpec((1, 8, d), lambda l, j: (l, 0, j)),
        out_shape=jax.ShapeDtypeStruct((n_layer, 8, nd), F32),
        compiler_params=_params("parallel", "parallel"),
    )(cvec, w_mod, b_mod.reshape(n_layer, 1, nd))


def _ffn_apply(x, mod, sub, gain, wup_ref, wdown_ref, n_chunk):
    d_ff = wdown_ref.shape[0]
    ck = d_ff // n_chunk
    xn = _norm_modulate(x, gain, mod[3 * sub:3 * sub + 1], mod[3 * sub + 1:3 * sub + 2]).astype(BF16)
    acc = jnp.zeros(x.shape, F32)
    for c in range(n_chunk):
        gate = jnp.dot(xn, wup_ref[:, c * ck:(c + 1) * ck], preferred_element_type=F32)
        up = jnp.dot(xn, wup_ref[:, d_ff + c * ck:d_ff + (c + 1) * ck], preferred_element_type=F32)
        h = (gate * _sigmoid(gate) * up).astype(BF16)
        acc = acc + jnp.dot(h, wdown_ref[c * ck:(c + 1) * ck, :], preferred_element_type=F32)
    return x + (0.5 * mod[3 * sub + 2:3 * sub + 3]) * acc


FFN_CHUNKS = 1


def _row_specs(d, first=0):
    return pl.BlockSpec((1, TM, d), lambda b, t: (b, t + first, 0))


def _mod_spec(d, first=0):
    return pl.BlockSpec((1, 1, 9, d), lambda b, t: (b, jnp.minimum(t + first, 1), 0, 0))


def _const_spec(shape):
    return pl.BlockSpec(shape, lambda b, t: (0,) * len(shape), pipeline_mode=pl.Buffered(1))


def _front_kernel(x_ref, mod_ref, gain_ref, wup_ref, wdown_ref, w_ref, qg_ref, kg_ref,
                  xo_ref, q_ref, k_ref, v_ref, pr_ref, *, attn_w):
    mod = mod_ref[0, 0]
    x = _ffn_apply(x_ref[0], mod, 0, gain_ref[0:1], wup_ref, wdown_ref, FFN_CHUNKS)
    xo_ref[0] = x
    xn = _norm_modulate(x, gain_ref[1:2], mod[3:4], mod[4:5]).astype(BF16)
    p = jnp.dot(xn, w_ref[...], preferred_element_type=F32)
    for col in range(attn_w // LANES):
        sl = slice(col * LANES, (col + 1) * LANES)
        qc = p[:, sl]
        kc = p[:, attn_w + col * LANES:attn_w + (col + 1) * LANES]
        qn = qc * lax.rsqrt(_head_sum(qc * qc) * (1.0 / HEAD_DIM) + RMS_EPS) * qg_ref[...]
        kn = kc * lax.rsqrt(_head_sum(kc * kc) * (1.0 / HEAD_DIM) + RMS_EPS) * kg_ref[...]
        q_ref[0, :, sl] = (qn * ATTN_SCALE).astype(BF16)
        k_ref[0, :, sl] = kn.astype(BF16)
    v_ref[0] = p[:, 2 * attn_w:3 * attn_w].astype(BF16)
    pr_ref[0] = p[:, 3 * attn_w:]


def _front(x, modtab, gains, wup, wdown, w_in, q_gain2, k_gain2, attn_w):
    nb, tt, d = x.shape
    d_in = w_in.shape[1]
    d_ff = wdown.shape[0]
    rw = d_in - 3 * attn_w
    return pl.pallas_call(
        functools.partial(_front_kernel, attn_w=attn_w),
        grid=(nb, tt // TM),
        in_specs=[_row_specs(d), _mod_spec(d), _const_spec((3, d)),
                  _const_spec((d, 2 * d_ff)), _const_spec((d_ff, d)), _const_spec((d, d_in)),
                  _const_spec((1, LANES)), _const_spec((1, LANES))],
        out_specs=[_row_specs(d), _row_specs(attn_w), _row_specs(attn_w), _row_specs(attn_w),
                   _row_specs(rw)],
        out_shape=[jax.ShapeDtypeStruct(x.shape, F32)]
        + [jax.ShapeDtypeStruct((nb, tt, attn_w), BF16)] * 3
        + [jax.ShapeDtypeStruct((nb, tt, rw), F32)],
        compiler_params=_params("parallel", "parallel"),
    )(x, modtab, gains, wup, wdown, w_in, q_gain2, k_gain2)


_TAB_PAD = LANES
_TAB_W = 11 * LANES


def _bias_layout(n_blk):
    rows = n_blk * Q_ROWS
    pats = []
    for blk in range(n_blk):
        key_row0 = int(np.clip(Q_ROWS * blk - NA_ROWS // 2, 0, rows - KEY_ROWS))
        pat = []
        for m in range(Q_ROWS):
            qi = Q_ROWS * blk + m
            start = int(np.clip(qi - NA_ROWS // 2, 0, rows - NA_ROWS))
            dst = (start - key_row0) * GRID_W
            off = _TAB_PAD + (start - qi + NA_ROWS - 1) * GRID_W - dst
            copy = (off % LANES) // HEAD_DIM
            base = off + copy * HEAD_DIM
            assert base >= 0 and base % LANES == 0 and base + KEY_ROWS * GRID_W <= _TAB_W
            pat.append((copy, base, dst))
        pats.append(tuple(pat))
    assert all(p == pats[1] for p in pats[1:n_blk - 1])
    return pats[0], pats[1], pats[n_blk - 1]


def _attn_kernel(q_ref, k_ref, v_ref, tab_ref, o_ref, bias_ref, *, ctx_len, n_blk):
    j = pl.program_id(2)
    q = q_ref[0]
    m0 = _head0_mask(q.shape)
    zero = jnp.zeros_like(q)
    k_ctx = k_ref[0, 0:ctx_len, :]
    v_ctx = v_ref[0, 0:ctx_len, :]
    nt = (((1,), (1,)), ((), ()))
    key_w = KEY_ROWS * GRID_W

    def head_q(hh):
        return jnp.where(m0, q, zero) if hh == 0 else jnp.where(m0, zero, q)

    lane = lax.broadcasted_iota(jnp.int32, (GRID_W, key_w), 1)
    for first_j, pat in zip((1, 2, n_blk), _bias_layout(n_blk)):
        @pl.when(j == first_j)
        def _(pat=pat):
            for hh in range(2):
                for m, (copy, base, dst) in enumerate(pat):
                    strip = tab_ref[hh, copy, :, base:base + key_w]
                    ok = jnp.logical_and(lane >= dst, lane < dst + NA_ROWS * GRID_W)
                    bias_ref[hh, m * GRID_W:(m + 1) * GRID_W, :] = jnp.where(ok, strip, NEG_INF)

    n_q = q.shape[0]
    part = n_q // ATTN_ROW_PARTS
    chains = [(hh, r0) for r0 in range(0, n_q, part) for hh in range(2)]
    qs = [head_q(hh)[r0:r0 + part] for hh, r0 in chains]
    nc = range(len(chains))

    def finish(outs):
        for r0 in range(0, n_q, part):
            o0, o1 = [outs[i] for i in nc if chains[i][1] == r0]
            o_ref[0, r0:r0 + part, :] = jnp.where(_head0_mask(o0.shape), o0, o1).astype(BF16)

    @pl.when(j == 0)
    def _():
        s = [lax.dot_general(qs[i], k_ctx, nt, preferred_element_type=F32) for i in nc]
        p = [jnp.exp(s[i] - jnp.max(s[i], axis=-1, keepdims=True)) for i in nc]
        o = [jnp.dot(p[i].astype(BF16), v_ctx, preferred_element_type=F32) for i in nc]
        finish([o[i] * (1.0 / jnp.sum(p[i], axis=-1, keepdims=True)) for i in nc])

    @pl.when(j > 0)
    def _():
        key_row0 = jnp.clip(Q_ROWS * (j - 1) - NA_ROWS // 2, 0, n_blk * Q_ROWS - KEY_ROWS)
        row0 = pl.multiple_of(ctx_len + key_row0 * GRID_W, GRID_W)
        k_loc = k_ref[0, pl.ds(row0, key_w), :]
        v_loc = v_ref[0, pl.ds(row0, key_w), :]
        s_loc = [lax.dot_general(qs[i], k_loc, nt, preferred_element_type=F32)
                 + bias_ref[chains[i][0], chains[i][1]:chains[i][1] + part, :] for i in nc]
        s_ctx = [lax.dot_general(qs[i], k_ctx, nt, preferred_element_type=F32) for i in nc]
        mx = [jnp.maximum(jnp.max(s_loc[i], axis=-1, keepdims=True),
                          jnp.max(s_ctx[i], axis=-1, keepdims=True)) for i in nc]
        p_loc = [jnp.exp(s_loc[i] - mx[i]) for i in nc]
        p_ctx = [jnp.exp(s_ctx[i] - mx[i]) for i in nc]
        den = [jnp.sum(p_loc[i], axis=-1, keepdims=True) + jnp.sum(p_ctx[i], axis=-1, keepdims=True)
               for i in nc]
        o = [jnp.dot(p_loc[i].astype(BF16), v_loc, preferred_element_type=F32)
             + jnp.dot(p_ctx[i].astype(BF16), v_ctx, preferred_element_type=F32) for i in nc]
        finish([o[i] * (1.0 / den[i]) for i in nc])


def _attention(q, k, v, table, ctx_len):
    nb, tt, w = q.shape
    n_blk = (tt - ctx_len) // (Q_ROWS * GRID_W)
    q_rows = Q_ROWS * GRID_W
    assert q_rows == ctx_len == TM and n_blk >= 3
    return pl.pallas_call(
        functools.partial(_attn_kernel, ctx_len=ctx_len, n_blk=n_blk),
        grid=(w // LANES, nb, n_blk + 1),
        in_specs=[pl.BlockSpec((1, q_rows, LANES), lambda hp, b, j: (b, j, hp)),
                  pl.BlockSpec((1, tt, LANES), lambda hp, b, j: (b, 0, hp)),
                  pl.BlockSpec((1, tt, LANES), lambda hp, b, j: (b, 0, hp)),
                  pl.BlockSpec((2, 2, GRID_W, _TAB_W), lambda hp, b, j: (hp, 0, 0, 0))],
        out_specs=pl.BlockSpec((1, q_rows, LANES), lambda hp, b, j: (b, j, hp)),
        out_shape=jax.ShapeDtypeStruct((nb, tt, w), BF16),
        scratch_shapes=[pltpu.VMEM((2, q_rows, KEY_ROWS * GRID_W), F32)],
        compiler_params=_params("parallel", "parallel", "arbitrary"),
    )(q, k, v, table)


def _bias_tables(na_bias):
    n_layer, n_head, n_dr, n_dc = na_bias.shape
    cols = np.arange(GRID_W)
    col_start = np.clip(cols - NA_COLS // 2, 0, GRID_W - NA_COLS)
    in_window = (cols[None, :] >= col_start[:, None]) & (cols[None, :] < col_start[:, None] + NA_COLS)
    dc = np.clip(cols[None, :] - cols[:, None], -(NA_COLS - 1), NA_COLS - 1) + (NA_COLS - 1)
    onehot = (dc[None] == np.arange(n_dc)[:, None, None]).astype(np.float32)
    tiles = jnp.einsum('lhrj,jqk->lhqrk', na_bias.astype(F32), onehot, precision=HIGHEST)
    tiles = jnp.where(in_window[None, None, :, None, :], tiles, NEG_INF)
    strip = tiles.reshape(n_layer, n_head, GRID_W, n_dr * GRID_W)
    copies = []
    for shift in (0, HEAD_DIM):
        left = _TAB_PAD + shift
        copies.append(jnp.pad(strip, ((0, 0), (0, 0), (0, 0), (left, _TAB_W - left - n_dr * GRID_W)),
                              constant_values=NEG_INF))
    return jnp.stack(copies, axis=2)


def _prep_kernel(p_ref, hp_ref, hn_ref, mu_ref, w0_ref, w2_ref, a0_ref, a2_ref, g2_ref,
                 kk_ref, ka_ref, u_ref,
                 r_ref, v_ref, kkn_ref, g_ref, bon_ref, lw_ref, kd_ref, bb_ref, *, hw, n_tile):
    t = pl.program_id(1)
    p = p_ref[0]
    rows = lax.broadcasted_iota(jnp.int32, (8, 1), 0)
    prev_ok = (t >= 2).astype(F32)
    next_ok = jnp.logical_and(t >= 1, t <= n_tile - 2).astype(F32)
    prev = pltpu.roll(p, 1, 0)
    nxt = pltpu.roll(p, TM - 1, 0)
    prev = jnp.concatenate(
        [jnp.where(rows == 0, hp_ref[0, 7:8, :] * prev_ok, prev[0:8]), prev[8:]], axis=0)
    nxt = jnp.concatenate(
        [nxt[:TM - 8], jnp.where(rows == 7, hn_ref[0, 0:1, :] * next_ok, nxt[TM - 8:])], axis=0)
    p = p + mu_ref[...] * (0.5 * (prev + nxt) - p)
    r = p[:, 0:hw]
    k = p[:, hw:2 * hw]
    v = p[:, 2 * hw:3 * hw]
    lora = p[:, 3 * hw:3 * hw + LANES]
    dg = p[:, 3 * hw + LANES:3 * hw + 2 * LANES]
    g_ref[0] = _bdot(_sigmoid(dg), g2_ref[...])
    r_ref[0] = r
    v_ref[0] = v
    tl = jnp.tanh(lora)
    ksum = jnp.zeros_like(k)
    kks = []
    for col in range(hw // LANES):
        sl = slice(col * LANES, (col + 1) * LANES)
        kc = k[:, sl] * kk_ref[:, sl]
        kks.append(kc * lax.rsqrt(jnp.maximum(_head_sum(kc * kc), L2_EPS)))
    kkn = jnp.concatenate(kks, axis=1)
    kkn_ref[0] = kkn
    for d in range(2):
        z = w0_ref[d:d + 1, :] + _bdot(tl, w2_ref[d])
        lw_ref[d, 0] = -np.float32(np.exp(-0.5)) * _sigmoid(z)
        a = _sigmoid(a0_ref[d:d + 1, :] + _bdot(lora, a2_ref[d]))
        kd = k * (1.0 + (a - 1.0) * ka_ref[...])
        kd_ref[d, 0] = kd
        bb_ref[d, 0] = kkn * a
        ksum = ksum + kd
    bon = r * ksum * u_ref[...]
    for col in range(hw // LANES):
        sl = slice(col * LANES, (col + 1) * LANES)
        bon_ref[0, :, sl] = _head_sum(bon[:, sl]) * v[:, sl]


def _rwkv_prepare(pr, shift_mu, w0, w2p, a0, a2p, g2, key_k, key_a, bonus_u, hw):
    nb, tt, rw = pr.shape
    n_tile = tt // TM
    sub = TM // 8
    tok = jax.ShapeDtypeStruct((nb, tt, hw), F32)
    tok2 = jax.ShapeDtypeStruct((2, nb, tt, hw), F32)
    tok_spec = _row_specs(hw)
    tok2_spec = pl.BlockSpec((2, 1, TM, hw), lambda b, t: (0, b, t, 0))
    return pl.pallas_call(
        functools.partial(_prep_kernel, hw=hw, n_tile=n_tile),
        grid=(nb, n_tile),
        in_specs=[_row_specs(rw),
                  pl.BlockSpec((1, 8, rw), lambda b, t: (b, jnp.maximum(t * sub - 1, 0), 0)),
                  pl.BlockSpec((1, 8, rw), lambda b, t: (b, jnp.minimum((t + 1) * sub, tt // 8 - 1), 0)),
                  _const_spec((1, rw)), _const_spec((2, hw)), _const_spec((2, LANES, hw)),
                  _const_spec((2, hw)), _const_spec((2, LANES, hw)), _const_spec((LANES, hw)),
                  _const_spec((1, hw)), _const_spec((1, hw)), _const_spec((1, hw))],
        out_specs=[tok_spec] * 5 + [tok2_spec] * 3,
        out_shape=[tok] * 5 + [tok2] * 3,
        compiler_params=_params("parallel", "parallel"),
    )(pr, pr, pr, shift_mu, w0, w2p, a0, a2p, g2, key_k, key_a, bonus_u)


def _stack_heads(x):
    m0 = _head0_mask(x.shape)
    zero = jnp.zeros_like(x)
    return jnp.concatenate([jnp.where(m0, x, zero), jnp.where(m0, zero, x)], axis=0)


def _scan_kernel(rf_ref, vf_ref, kkf_ref, lwf_ref, kdf_ref, bbf_ref,
                 rb_ref, vb_ref, kkb_ref, lwb_ref, kdb_ref, bbb_ref,
                 of_ref, ob_ref, st_ref, *, hw):
    n_pair = hw // LANES
    c2 = 2 * CHUNK

    @pl.when(pl.program_id(1) == 0)
    def _():
        st_ref[...] = jnp.zeros(st_ref.shape, F32)

    ti = lax.broadcasted_iota(jnp.int32, (CHUNK, CHUNK), 0)
    tj = lax.broadcasted_iota(jnp.int32, (CHUNK, CHUNK), 1)
    ri = lax.broadcasted_iota(jnp.int32, (c2, c2), 0)
    rj = lax.broadcasted_iota(jnp.int32, (c2, c2), 1)
    same = (ri // CHUNK) == (rj // CHUNK)
    blk_r, blk_c = ri % CHUNK, rj % CHUNK
    eye = (ri == rj).astype(F32)
    li = lax.broadcasted_iota(jnp.int32, (LANES, LANES), 0)
    lj = lax.broadcasted_iota(jnp.int32, (LANES, LANES), 1)
    eye_l = (li == lj).astype(F32)
    sls = [slice(pr * LANES, (pr + 1) * LANES) for pr in range(n_pair)]

    ops = []
    for bi in range(rf_ref.shape[0]):
        for d, (r_ref, v_ref, kk_ref, lw_ref, kd_ref, bb_ref) in enumerate(
                ((rf_ref, vf_ref, kkf_ref, lwf_ref, kdf_ref, bbf_ref),
                 (rb_ref, vb_ref, kkb_ref, lwb_ref, kdb_ref, bbb_ref))):
            incl1 = (tj <= ti) if d == 0 else (tj >= ti)
            before = (blk_c - blk_r) if d == 0 else (blk_r - blk_c)
            incl = jnp.logical_and(same, before <= 0).astype(F32)
            strict = jnp.logical_and(same, before < 0).astype(F32)
            lw = lw_ref[0, bi]
            incl1 = incl1.astype(F32).astype(BF16)
            lw1 = lw.astype(BF16)
            lw2, lw3 = _split(lw - lw1.astype(F32))
            cum = (_mm(incl1, lw3) + _mm(incl1, lw2)) + _mm(incl1, lw1)
            cum_prev = cum - lw
            tot = jnp.sum(lw, axis=0, keepdims=True)
            mid = 0.5 * tot
            r, v, kk = r_ref[bi], v_ref[bi], kk_ref[bi]
            kd, bb = kd_ref[0, bi], bb_ref[0, bi]
            e_inv = jnp.exp(mid - cum)
            e_end = jnp.exp(tot - cum)
            full = dict(a_hat=kk * jnp.exp(cum_prev - mid), r_hat=r * jnp.exp(cum - mid),
                        k_til=kd * e_inv, b_til=bb * e_inv, v=v,
                        a_abs=kk * jnp.exp(cum_prev), r_abs=r * jnp.exp(cum),
                        k_end=kd * e_end, b_end=bb * e_end)
            e_tot = jnp.exp(tot)
            for p in range(n_pair):
                op = {name: _stack_heads(x[:, sls[p]]) for name, x in full.items()}
                op.update(incl=incl, strict=strict, e_tot=e_tot[:, sls[p]], bi=bi, d=d, p=p)
                ops.append(op)

    ch = range(len(ops))
    bf = lambda name: [ops[i][name].astype(BF16) for i in ch]
    a_s, r_s, k_s, b_s, v_s = bf("a_hat"), bf("r_hat"), bf("k_til"), bf("b_til"), bf("v")
    ar_s = [jnp.concatenate([a_s[i], r_s[i]], axis=0) for i in ch]
    ar_b = [_mm(ar_s[i], b_s[i], _NT) for i in ch]
    ar_k = [_mm(ar_s[i], k_s[i], _NT) for i in ch]
    low = [ops[i]["strict"] * ar_b[i][:c2] for i in ch]
    m_rb = [(ops[i]["incl"] * ar_b[i][c2:]).astype(BF16) for i in ch]
    m_kk = [jnp.concatenate([ops[i]["strict"] * ar_k[i][:c2], ops[i]["incl"] * ar_k[i][c2:]],
                            axis=0).astype(BF16) for i in ch]
    t_inv = None
    s = 1
    while s < CHUNK:
        join = jnp.logical_and(blk_r // (2 * s) == blk_c // (2 * s), blk_r // s != blk_c // s)
        join = join.astype(F32)
        if t_inv is None:
            t_inv = [eye - join * low[i] for i in ch]
        else:
            d2 = [t_inv[i].astype(BF16) for i in ch]
            ed = [_mm((join * low[i]).astype(BF16), d2[i]).astype(BF16) for i in ch]
            t_inv = [t_inv[i] - _mm(d2[i], ed[i]) for i in ch]
        s *= 2
    mv = [_mm(m_kk[i], v_s[i]) for i in ch]
    au = [_mm(t_inv[i].astype(BF16),
              jnp.concatenate([ops[i]["a_abs"], mv[i][:c2]], axis=1).astype(BF16)).astype(BF16)
          for i in ch]
    qo = [jnp.concatenate([ops[i]["r_abs"], mv[i][c2:]], axis=1) - _mm(m_rb[i], au[i]) for i in ch]
    ba = [_mm(ops[i]["b_end"].astype(BF16), au[i], _TN) for i in ch]
    kv = [_mm(ops[i]["k_end"].astype(BF16), v_s[i], _TN) for i in ch]
    where = [(ops[i]["bi"], ops[i]["d"], ops[i]["p"]) for i in ch]
    st = [_split(st_ref[where[i]]) for i in ch]
    g2 = [_split(eye_l * ops[i]["e_tot"] - ba[i][:, :LANES]) for i in ch]
    prod = [_mm(jnp.concatenate([qo[i][:, :LANES].astype(BF16), g2[i][0], g2[i][1]], axis=0),
                st[i][0]) for i in ch]
    g_lo = [_mm(g2[i][0], st[i][1]) for i in ch]
    for i in ch:
        bi, d, p = where[i]
        o_s = prod[i][:c2] + qo[i][:, LANES:]
        o_ref = of_ref if d == 0 else ob_ref
        o_ref[bi, :, sls[p]] = o_s[:CHUNK] + o_s[CHUNK:]
    for i in ch:
        st_ref[where[i]] = ((g_lo[i] + prod[i][c2 + LANES:]) + prod[i][c2:c2 + LANES]
                            + (kv[i] - ba[i][:, LANES:]))


def _rwkv_scan(r, v, kk, lw, kd, bb, ctx_len):
    nb, tt, hw = r.shape
    n_chunk = tt // CHUNK
    n_ctx = ctx_len // CHUNK

    def back(c):
        return jnp.where(c < n_ctx, n_ctx - 1 - c, n_chunk + n_ctx - 1 - c)

    sb = SCAN_BATCH
    assert nb % sb == 0
    fwd = pl.BlockSpec((sb, CHUNK, hw), lambda b, c: (b, c, 0))
    bwd = pl.BlockSpec((sb, CHUNK, hw), lambda b, c: (b, back(c), 0))
    fwd_d = pl.BlockSpec((1, sb, CHUNK, hw), lambda b, c: (0, b, c, 0))
    bwd_d = pl.BlockSpec((1, sb, CHUNK, hw), lambda b, c: (1, b, back(c), 0))
    return pl.pallas_call(
        functools.partial(_scan_kernel, hw=hw),
        grid=(nb // sb, n_chunk),
        in_specs=[fwd, fwd, fwd, fwd_d, fwd_d, fwd_d, bwd, bwd, bwd, bwd_d, bwd_d, bwd_d],
        out_specs=[fwd, bwd],
        out_shape=[jax.ShapeDtypeStruct((nb, tt, hw), F32)] * 2,
        scratch_shapes=[pltpu.VMEM((sb, 2, hw // LANES, LANES, LANES), F32)],
        compiler_params=_params("parallel", "arbitrary"),
    )(r, v, kk, lw, kd, bb, r, v, kk, lw, kd, bb)


def _back_kernel(x_ref, mod_ref, gain_ref, attn_ref, of_ref, ob_ref, bon_ref, g_ref, lg_ref, lb_ref,
                 w_ref, wup_ref, wdown_ref, o_ref, *, hw):
    mod = mod_ref[0, 0]
    o = of_ref[0] + ob_ref[0]
    cols = []
    for col in range(hw // LANES):
        sl = slice(col * LANES, (col + 1) * LANES)
        oc = o[:, sl]
        dev = oc - _head_sum(oc) * (1.0 / HEAD_DIM)
        var = _head_sum(dev * dev) * (1.0 / HEAD_DIM)
        cols.append(dev * lax.rsqrt(var + LNX_EPS))
    on = jnp.concatenate(cols, axis=1) * lg_ref[...] + lb_ref[...]
    rw = ((on + bon_ref[0]) * g_ref[0]).astype(BF16)
    aw = attn_ref.shape[2]
    y = (jnp.dot(attn_ref[0], w_ref[0:aw, :], preferred_element_type=F32)
         + jnp.dot(rw, w_ref[aw:, :], preferred_element_type=F32))
    x = x_ref[0] + mod[5:6] * y
    o_ref[0] = _ffn_apply(x, mod, 2, gain_ref[2:3], wup_ref, wdown_ref, FFN_CHUNKS)


def _back(x, modtab, gains, attn, o_fwd, o_bwd, bon, g, lnx_gain, lnx_bias, w_out, wup, wdown,
          first_tile):
    nb, tt, d = x.shape
    hw = bon.shape[2]
    aw = attn.shape[2]
    d_ff = wdown.shape[0]
    n_tile = tt // TM - first_tile
    rows = lambda w: _row_specs(w, first_tile)
    return pl.pallas_call(
        functools.partial(_back_kernel, hw=hw),
        grid=(nb, n_tile),
        in_specs=[rows(d), _mod_spec(d, first_tile), _const_spec((3, d)), rows(aw), rows(hw),
                  rows(hw), rows(hw), rows(hw), _const_spec((1, hw)), _const_spec((1, hw)),
                  _const_spec((d, d)), _const_spec((d, 2 * d_ff)), _const_spec((d_ff, d))],
        out_specs=_row_specs(d),
        out_shape=jax.ShapeDtypeStruct((nb, n_tile * TM, d), F32),
        compiler_params=_params("parallel", "parallel"),
    )(x, modtab, gains, attn, o_fwd, o_bwd, bon, g, lnx_gain, lnx_bias, w_out, wup, wdown)


def kernel(x, c, ctx, c_ctx, w_mod, b_mod, norm_gain, ffn_up, ffn_down, w_in, q_gain, k_gain,
           na_bias, shift_mu, decay_w0, decay_w2, iclr_a0, iclr_a2, gate_g2, key_k, key_a,
           bonus_u, lnx_gain, lnx_bias, w_out):
    nb, seq, d = x.shape
    ctx_len = ctx.shape[1]
    depth = w_mod.shape[0]
    n_heads = na_bias.shape[1]
    attn_w = n_heads * HEAD_DIM
    hw = decay_w0.shape[2]
    lora_w = decay_w2.shape[2]
    assert ctx_len == TM and seq % (Q_ROWS * GRID_W) == 0 and attn_w % LANES == 0 and hw % LANES == 0
    assert lora_w + iclr_a2.shape[2] == LANES and gate_g2.shape[1] == LANES and nb + 1 <= 8

    cvec = jnp.zeros((8, d), F32).at[:nb].set(c).at[nb].set(c_ctx)
    mods = _modulation(cvec, w_mod, b_mod).reshape(depth, 8, 9, d)
    modtab = jnp.stack([jnp.broadcast_to(mods[:, nb:nb + 1], (depth, nb, 9, d)), mods[:, :nb]], axis=2)

    up_b, down_b = ffn_up.astype(BF16), ffn_down.astype(BF16)
    w_in_b, w_out_b = w_in.astype(BF16), w_out.astype(BF16)
    zpad = jnp.zeros_like(decay_w2)
    w2p = jnp.concatenate([decay_w2, zpad], axis=2)
    a2p = jnp.concatenate([zpad, iclr_a2], axis=2)
    bias_tabs = _bias_tables(na_bias)

    xs = jnp.concatenate([ctx, x], axis=1)
    for l in range(depth):
        mt = modtab[l]
        xs, q, k, v, pr = _front(xs, mt, norm_gain[l], up_b[l, 0], down_b[l, 0], w_in_b[l],
                                 jnp.tile(q_gain[l], 2)[None], jnp.tile(k_gain[l], 2)[None], attn_w)
        attn = _attention(q, k, v, bias_tabs[l], ctx_len)
        r, vr, kk, g, bon, lw, kd, bb = _rwkv_prepare(
            pr, shift_mu[l][None], decay_w0[l], w2p[l], iclr_a0[l], a2p[l], gate_g2[l],
            key_k[l][None], key_a[l][None], bonus_u[l].reshape(1, hw), hw)
        o_fwd, o_bwd = _rwkv_scan(r, vr, kk, lw, kd, bb, ctx_len)
        xs = _back(xs, mt, norm_gain[l], attn, o_fwd, o_bwd, bon, g, lnx_gain[l][None],
                   lnx_bias[l][None], w_out_b[l], up_b[l, 1], down_b[l, 1],
                   first_tile=ctx_len // TM if l == depth - 1 else 0)
    return xs
```

```python
import functools

import jax
import jax.numpy as jnp
import numpy as np
from jax import lax
from jax.experimental import pallas as pl
from jax.experimental.pallas import tpu as pltpu

F32 = jnp.float32
BF16 = jnp.bfloat16
HIGHEST = lax.Precision.HIGHEST

GRID_W = 64
NA_ROWS = 8
NA_COLS = 16
HEAD_DIM = 64
LANES = 128
TM = 256
CHUNK = 64
SCAN_BATCH = 2
Q_ROWS = 8
KEY_ROWS = Q_ROWS + NA_ROWS - 1
RMS_EPS = 1e-6
LNX_EPS = 64e-5
L2_EPS = 1e-24
NEG_INF = -1e30
ATTN_SCALE = HEAD_DIM ** -0.5
VMEM_LIMIT = 56 * 1024 * 1024


def _params(*sem):
    return pltpu.CompilerParams(dimension_semantics=sem, vmem_limit_bytes=VMEM_LIMIT)


def _bdot(a, b):
    return jnp.dot(a.astype(BF16), b.astype(BF16), preferred_element_type=F32)


_NN = (((1,), (0,)), ((), ()))
_NT = (((1,), (1,)), ((), ()))
_TN = (((0,), (0,)), ((), ()))


def _mm(a, b, dims=_NN):
    return lax.dot_general(a, b, dims, preferred_element_type=F32)


def _split(a):
    hi = a.astype(BF16)
    return hi, (a - hi.astype(F32)).astype(BF16)


def _mm3(a2, b2):
    (ah, al), (bh, bl) = a2, b2
    return (_mm(ah, bl) + _mm(al, bh)) + _mm(ah, bh)


def _sigmoid(x):
    return 1.0 / (1.0 + jnp.exp(-x))


def _head0_mask(shape):
    return lax.broadcasted_iota(jnp.int32, shape, len(shape) - 1) < HEAD_DIM


def _head_sum(z):
    m0 = _head0_mask(z.shape)
    s0 = jnp.sum(jnp.where(m0, z, 0.0), axis=-1, keepdims=True)
    s1 = jnp.sum(jnp.where(m0, 0.0, z), axis=-1, keepdims=True)
    return jnp.where(m0, s0, s1)


def _norm_modulate(x, gain, shift, scale):
    y = x * lax.rsqrt(jnp.mean(x * x, axis=-1, keepdims=True) + RMS_EPS)
    return (y * gain) * (1.0 + scale) + shift


def _mod_kernel(c_ref, w_ref, b_ref, o_ref):
    c = c_ref[...]
    o_ref[0] = _bdot(c * _sigmoid(c), w_ref[0]) + b_ref[0]


def _modulation(cvec, w_mod, b_mod):
    n_layer, d, nd = w_mod.shape
    return pl.pallas_call(
        _mod_kernel,
        grid=(n_layer, nd // d),
        in_specs=[pl.BlockSpec((8, d), lambda l, j: (0, 0)),
                  pl.BlockSpec((1, d, d), lambda l, j: (l, 0, j)),
                  pl.BlockSpec((1, 1, d), lambda l, j: (l, 0, j))],
        out_specs=pl.BlockSpec((1, 8, d), lambda l, j: (l, 0, j)),
        out_shape=jax.ShapeDtypeStruct((n_layer, 8, nd), F32),
        compiler_params=_params("parallel", "parallel"),
    )(cvec, w_mod, b_mod.reshape(n_layer, 1, nd))


def _ffn_apply(x, mod, sub, gain, wup_ref, wdown_ref, n_chunk):
    d_ff = wdown_ref.shape[0]
    ck = d_ff // n_chunk
    xn = _norm_modulate(x, gain, mod[3 * sub:3 * sub + 1], mod[3 * sub + 1:3 * sub + 2]).astype(BF16)
    acc = jnp.zeros(x.shape, F32)
    for c in range(n_chunk):
        gate = jnp.dot(xn, wup_ref[:, c * ck:(c + 1) * ck], preferred_element_type=F32)
        up = jnp.dot(xn, wup_ref[:, d_ff + c * ck:d_ff + (c + 1) * ck], preferred_element_type=F32)
        h = (gate * _sigmoid(gate) * up).astype(BF16)
        acc = acc + jnp.dot(h, wdown_ref[c * ck:(c + 1) * ck, :], preferred_element_type=F32)
    return x + (0.5 * mod[3 * sub + 2:3 * sub + 3]) * acc


FFN_CHUNKS = 1


def _row_specs(d, first=0):
    return pl.BlockSpec((1, TM, d), lambda b, t: (b, t + first, 0))


def _mod_spec(d, first=0):
    return pl.BlockSpec((1, 1, 9, d), lambda b, t: (b, jnp.minimum(t + first, 1), 0, 0))


def _const_spec(shape):
    return pl.BlockSpec(shape, lambda b, t: (0,) * len(shape), pipeline_mode=pl.Buffered(1))


def _stack_spec(stack, *lead):
    tail = stack.shape[len(lead):]
    return pl.BlockSpec((None,) * len(lead) + tail, lambda b, t: lead + (0,) * len(tail),
                        pipeline_mode=pl.Buffered(1))


def _front_kernel(x_ref, mod_ref, gain_ref, wup_ref, wdown_ref, w_ref, qg_ref, kg_ref,
                  xo_ref, q_ref, k_ref, v_ref, pr_ref, *, attn_w):
    mod = mod_ref[0, 0]
    x = _ffn_apply(x_ref[0], mod, 0, gain_ref[0:1], wup_ref, wdown_ref, FFN_CHUNKS)
    xo_ref[0] = x
    xn = _norm_modulate(x, gain_ref[1:2], mod[3:4], mod[4:5]).astype(BF16)
    p = jnp.dot(xn, w_ref[...], preferred_element_type=F32)
    for col in range(attn_w // LANES):
        sl = slice(col * LANES, (col + 1) * LANES)
        qc = p[:, sl]
        kc = p[:, attn_w + col * LANES:attn_w + (col + 1) * LANES]
        qn = qc * lax.rsqrt(_head_sum(qc * qc) * (1.0 / HEAD_DIM) + RMS_EPS) * qg_ref[...]
        kn = kc * lax.rsqrt(_head_sum(kc * kc) * (1.0 / HEAD_DIM) + RMS_EPS) * kg_ref[...]
        q_ref[0, :, sl] = (qn * ATTN_SCALE).astype(BF16)
        k_ref[0, :, sl] = kn.astype(BF16)
    v_ref[0] = p[:, 2 * attn_w:3 * attn_w].astype(BF16)
    pr_ref[0] = p[:, 3 * attn_w:]


def _front(x, modtab, gains, wup, wdown, w_in, layer, q_gain2, k_gain2, attn_w):
    nb, tt, d = x.shape
    d_in = w_in.shape[-1]
    rw = d_in - 3 * attn_w
    return pl.pallas_call(
        functools.partial(_front_kernel, attn_w=attn_w),
        grid=(nb, tt // TM),
        in_specs=[_row_specs(d), _mod_spec(d), _const_spec((3, d)),
                  _stack_spec(wup, layer, 0), _stack_spec(wdown, layer, 0),
                  _stack_spec(w_in, layer),
                  _const_spec((1, LANES)), _const_spec((1, LANES))],
        out_specs=[_row_specs(d), _row_specs(attn_w), _row_specs(attn_w), _row_specs(attn_w),
                   _row_specs(rw)],
        out_shape=[jax.ShapeDtypeStruct(x.shape, F32)]
        + [jax.ShapeDtypeStruct((nb, tt, attn_w), BF16)] * 3
        + [jax.ShapeDtypeStruct((nb, tt, rw), F32)],
        compiler_params=_params("parallel", "parallel"),
    )(x, modtab, gains, wup, wdown, w_in, q_gain2, k_gain2)


_TAB_PAD = 4 * LANES
_TAB_W = 16 * LANES


def _bias_layout(n_blk):
    rows = n_blk * Q_ROWS
    pats = []
    for blk in range(n_blk):
        key_row0 = int(np.clip(Q_ROWS * blk - NA_ROWS // 2, 0, rows - KEY_ROWS))
        pat = []
        for m in range(Q_ROWS):
            qi = Q_ROWS * blk + m
            start = int(np.clip(qi - NA_ROWS // 2, 0, rows - NA_ROWS))
            dst = (start - key_row0) * GRID_W
            off = _TAB_PAD + (start - qi + NA_ROWS - 1) * GRID_W - dst
            copy = (off % LANES) // HEAD_DIM
            base = off + copy * HEAD_DIM
            assert base >= 0 and base % LANES == 0 and base + KEY_ROWS * GRID_W <= _TAB_W
            pat.append((copy, base, dst))
        pats.append(tuple(pat))
    assert all(p == pats[1] for p in pats[1:n_blk - 1])
    return pats[0], pats[1], pats[n_blk - 1]


def _attn_kernel(q_ref, k_ref, v_ref, tab_ref, o_ref, bias_ref, *, ctx_len, n_blk):
    j = pl.program_id(2)
    k_ctx = k_ref[0, 0:ctx_len, :]
    v_ctx = v_ref[0, 0:ctx_len, :]
    nt = (((1,), (1,)), ((), ()))
    key_w = KEY_ROWS * GRID_W
    q_rows = Q_ROWS * GRID_W
    nc = range(2)

    def head_qs(q):
        m0 = _head0_mask(q.shape)
        zero = jnp.zeros_like(q)
        return [jnp.where(m0, q, zero), jnp.where(m0, zero, q)]

    def merge(outs):
        return jnp.where(_head0_mask(outs[0].shape), outs[0], outs[1]).astype(BF16)

    lane = lax.broadcasted_iota(jnp.int32, (GRID_W, key_w), 1)
    for first_j, pat in zip((1, 2, n_blk), _bias_layout(n_blk)):
        @pl.when(j == first_j)
        def _(pat=pat):
            for hh in range(2):
                for m, (copy, base, dst) in enumerate(pat):
                    strip = tab_ref[hh, copy, :, base:base + key_w]
                    ok = jnp.logical_and(lane >= dst, lane < dst + NA_ROWS * GRID_W)
                    bias_ref[hh, m * GRID_W:(m + 1) * GRID_W, :] = jnp.where(ok, strip, NEG_INF)

    @pl.when(j == 0)
    def _():
        qs = head_qs(q_ref[0, 0:ctx_len, :])
        s = [lax.dot_general(qs[i], k_ctx, nt, preferred_element_type=F32) for i in nc]
        p = [jnp.exp(s[i] - jnp.max(s[i], axis=-1, keepdims=True)) for i in nc]
        o = [jnp.dot(p[i].astype(BF16), v_ctx, preferred_element_type=F32) for i in nc]
        o_ref[0, 0:ctx_len, :] = merge(
            [o[i] * (1.0 / jnp.sum(p[i], axis=-1, keepdims=True)) for i in nc])

    @pl.when(j > 0)
    def _():
        q_row0 = pl.multiple_of(ctx_len + (j - 1) * q_rows, GRID_W)
        key_row0 = jnp.clip(Q_ROWS * (j - 1) - NA_ROWS // 2, 0, n_blk * Q_ROWS - KEY_ROWS)
        row0 = pl.multiple_of(ctx_len + key_row0 * GRID_W, GRID_W)
        qs = head_qs(q_ref[0, pl.ds(q_row0, q_rows), :])
        k_loc = k_ref[0, pl.ds(row0, key_w), :]
        v_loc = v_ref[0, pl.ds(row0, key_w), :]
        s_loc = [lax.dot_general(qs[i], k_loc, nt, preferred_element_type=F32) + bias_ref[i]
                 for i in nc]
        s_ctx = [lax.dot_general(qs[i], k_ctx, nt, preferred_element_type=F32) for i in nc]
        mx = [jnp.maximum(jnp.max(s_loc[i], axis=-1, keepdims=True),
                          jnp.max(s_ctx[i], axis=-1, keepdims=True)) for i in nc]
        p_loc = [jnp.exp(s_loc[i] - mx[i]) for i in nc]
        p_ctx = [jnp.exp(s_ctx[i] - mx[i]) for i in nc]
        den = [jnp.sum(p_loc[i], axis=-1, keepdims=True) + jnp.sum(p_ctx[i], axis=-1, keepdims=True)
               for i in nc]
        o = [jnp.dot(p_loc[i].astype(BF16), v_loc, preferred_element_type=F32)
             + jnp.dot(p_ctx[i].astype(BF16), v_ctx, preferred_element_type=F32) for i in nc]
        o_ref[0, pl.ds(q_row0, q_rows), :] = merge([o[i] * (1.0 / den[i]) for i in nc])


def _attention(q, k, v, table, ctx_len):
    nb, tt, w = q.shape
    q_rows = Q_ROWS * GRID_W
    n_blk = (tt - ctx_len) // q_rows
    assert (tt - ctx_len) % q_rows == 0 and ctx_len % GRID_W == 0 and n_blk >= 3
    whole = pl.BlockSpec((1, tt, LANES), lambda hp, b, j: (b, 0, hp))
    return pl.pallas_call(
        functools.partial(_attn_kernel, ctx_len=ctx_len, n_blk=n_blk),
        grid=(w // LANES, nb, n_blk + 1),
        in_specs=[whole, whole, whole,
                  pl.BlockSpec((2, 2, GRID_W, _TAB_W), lambda hp, b, j: (hp, 0, 0, 0))],
        out_specs=whole,
        out_shape=jax.ShapeDtypeStruct((nb, tt, w), BF16),
        scratch_shapes=[pltpu.VMEM((2, q_rows, KEY_ROWS * GRID_W), F32)],
        compiler_params=_params("parallel", "parallel", "arbitrary"),
    )(q, k, v, table)


def _bias_tables(na_bias):
    n_layer, n_head, n_dr, n_dc = na_bias.shape
    cols = np.arange(GRID_W)
    col_start = np.clip(cols - NA_COLS // 2, 0, GRID_W - NA_COLS)
    in_window = (cols[None, :] >= col_start[:, None]) & (cols[None, :] < col_start[:, None] + NA_COLS)
    dc = np.clip(cols[None, :] - cols[:, None], -(NA_COLS - 1), NA_COLS - 1) + (NA_COLS - 1)
    onehot = (dc[None] == np.arange(n_dc)[:, None, None]).astype(np.float32)
    tiles = jnp.einsum('lhrj,jqk->lhqrk', na_bias.astype(F32), onehot, precision=HIGHEST)
    tiles = jnp.where(in_window[None, None, :, None, :], tiles, NEG_INF)
    strip = tiles.reshape(n_layer, n_head, GRID_W, n_dr * GRID_W)
    copies = []
    for shift in (0, HEAD_DIM):
        left = _TAB_PAD + shift
        copies.append(jnp.pad(strip, ((0, 0), (0, 0), (0, 0), (left, _TAB_W - left - n_dr * GRID_W)),
                              constant_values=NEG_INF))
    return jnp.stack(copies, axis=2)


def _prep_kernel(p_ref, hp_ref, hn_ref, mu_ref, w0_ref, w2_ref, a0_ref, a2_ref, g2_ref,
                 kk_ref, ka_ref, u_ref,
                 r_ref, v_ref, kkn_ref, g_ref, bon_ref, lw_ref, kd_ref, bb_ref, *, hw, n_tile):
    t = pl.program_id(1)
    p = p_ref[0]
    rows = lax.broadcasted_iota(jnp.int32, (8, 1), 0)
    prev_ok = (t >= 2).astype(F32)
    next_ok = jnp.logical_and(t >= 1, t <= n_tile - 2).astype(F32)
    prev = pltpu.roll(p, 1, 0)
    nxt = pltpu.roll(p, TM - 1, 0)
    prev = jnp.concatenate(
        [jnp.where(rows == 0, hp_ref[0, 7:8, :] * prev_ok, prev[0:8]), prev[8:]], axis=0)
    nxt = jnp.concatenate(
        [nxt[:TM - 8], jnp.where(rows == 7, hn_ref[0, 0:1, :] * next_ok, nxt[TM - 8:])], axis=0)
    p = p + mu_ref[...] * (0.5 * (prev + nxt) - p)
    r = p[:, 0:hw]
    k = p[:, hw:2 * hw]
    v = p[:, 2 * hw:3 * hw]
    lora = p[:, 3 * hw:3 * hw + LANES]
    dg = p[:, 3 * hw + LANES:3 * hw + 2 * LANES]
    g_ref[0] = _bdot(_sigmoid(dg), g2_ref[...])
    r_ref[0] = r
    v_ref[0] = v
    tl = jnp.tanh(lora)
    ksum = jnp.zeros_like(k)
    kks = []
    for col in range(hw // LANES):
        sl = slice(col * LANES, (col + 1) * LANES)
        kc = k[:, sl] * kk_ref[:, sl]
        kks.append(kc * lax.rsqrt(jnp.maximum(_head_sum(kc * kc), L2_EPS)))
    kkn = jnp.concatenate(kks, axis=1)
    kkn_ref[0] = kkn
    for d in range(2):
        z = w0_ref[d:d + 1, :] + _bdot(tl, w2_ref[d])
        lw_ref[d, 0] = -np.float32(np.exp(-0.5)) * _sigmoid(z)
        a = _sigmoid(a0_ref[d:d + 1, :] + _bdot(lora, a2_ref[d]))
        kd = k * (1.0 + (a - 1.0) * ka_ref[...])
        kd_ref[d, 0] = kd
        bb_ref[d, 0] = kkn * a
        ksum = ksum + kd
    bon = r * ksum * u_ref[...]
    for col in range(hw // LANES):
        sl = slice(col * LANES, (col + 1) * LANES)
        bon_ref[0, :, sl] = _head_sum(bon[:, sl]) * v[:, sl]


def _rwkv_prepare(pr, shift_mu, w0, w2p, a0, a2p, g2, key_k, key_a, bonus_u, hw):
    nb, tt, rw = pr.shape
    n_tile = tt // TM
    sub = TM // 8
    tok = jax.ShapeDtypeStruct((nb, tt, hw), F32)
    tok2 = jax.ShapeDtypeStruct((2, nb, tt, hw), F32)
    tok_spec = _row_specs(hw)
    tok2_spec = pl.BlockSpec((2, 1, TM, hw), lambda b, t: (0, b, t, 0))
    return pl.pallas_call(
        functools.partial(_prep_kernel, hw=hw, n_tile=n_tile),
        grid=(nb, n_tile),
        in_specs=[_row_specs(rw),
                  pl.BlockSpec((1, 8, rw), lambda b, t: (b, jnp.maximum(t * sub - 1, 0), 0)),
                  pl.BlockSpec((1, 8, rw), lambda b, t: (b, jnp.minimum((t + 1) * sub, tt // 8 - 1), 0)),
                  _const_spec((1, rw)), _const_spec((2, hw)), _const_spec((2, LANES, hw)),
                  _const_spec((2, hw)), _const_spec((2, LANES, hw)), _const_spec((LANES, hw)),
                  _const_spec((1, hw)), _const_spec((1, hw)), _const_spec((1, hw))],
        out_specs=[tok_spec] * 5 + [tok2_spec] * 3,
        out_shape=[tok] * 5 + [tok2] * 3,
        compiler_params=_params("parallel", "parallel"),
    )(pr, pr, pr, shift_mu, w0, w2p, a0, a2p, g2, key_k, key_a, bonus_u)


def _stack_heads(x):
    m0 = _head0_mask(x.shape)
    zero = jnp.zeros_like(x)
    return jnp.concatenate([jnp.where(m0, x, zero), jnp.where(m0, zero, x)], axis=0)


def _scan_kernel(rf_ref, vf_ref, kkf_ref, lwf_ref, kdf_ref, bbf_ref,
                 rb_ref, vb_ref, kkb_ref, lwb_ref, kdb_ref, bbb_ref,
                 of_ref, ob_ref, st_ref, *, hw):
    n_pair = hw // LANES
    c2 = 2 * CHUNK

    @pl.when(pl.program_id(1) == 0)
    def _():
        st_ref[...] = jnp.zeros(st_ref.shape, F32)

    ti = lax.broadcasted_iota(jnp.int32, (CHUNK, CHUNK), 0)
    tj = lax.broadcasted_iota(jnp.int32, (CHUNK, CHUNK), 1)
    ri = lax.broadcasted_iota(jnp.int32, (c2, c2), 0)
    rj = lax.broadcasted_iota(jnp.int32, (c2, c2), 1)
    same = (ri // CHUNK) == (rj // CHUNK)
    blk_r, blk_c = ri % CHUNK, rj % CHUNK
    eye = (ri == rj).astype(F32)
    li = lax.broadcasted_iota(jnp.int32, (LANES, LANES), 0)
    lj = lax.broadcasted_iota(jnp.int32, (LANES, LANES), 1)
    eye_l = (li == lj).astype(F32)
    sls = [slice(pr * LANES, (pr + 1) * LANES) for pr in range(n_pair)]

    ops = []
    for bi in range(rf_ref.shape[0]):
        for d, (r_ref, v_ref, kk_ref, lw_ref, kd_ref, bb_ref) in enumerate(
                ((rf_ref, vf_ref, kkf_ref, lwf_ref, kdf_ref, bbf_ref),
                 (rb_ref, vb_ref, kkb_ref, lwb_ref, kdb_ref, bbb_ref))):
            incl1 = (tj <= ti) if d == 0 else (tj >= ti)
            before = (blk_c - blk_r) if d == 0 else (blk_r - blk_c)
            incl = jnp.logical_and(same, before <= 0).astype(F32)
            strict = jnp.logical_and(same, before < 0).astype(F32)
            lw = lw_ref[0, bi]
            incl1 = incl1.astype(F32).astype(BF16)
            lw1 = lw.astype(BF16)
            lw2, lw3 = _split(lw - lw1.astype(F32))
            cum = (_mm(incl1, lw3) + _mm(incl1, lw2)) + _mm(incl1, lw1)
            cum_prev = cum - lw
            tot = jnp.sum(lw, axis=0, keepdims=True)
            mid = 0.5 * tot
            r, v, kk = r_ref[bi], v_ref[bi], kk_ref[bi]
            kd, bb = kd_ref[0, bi], bb_ref[0, bi]
            e_inv = jnp.exp(mid - cum)
            e_end = jnp.exp(tot - cum)
            full = dict(a_hat=kk * jnp.exp(cum_prev - mid), r_hat=r * jnp.exp(cum - mid),
                        k_til=kd * e_inv, b_til=bb * e_inv, v=v,
                        a_abs=kk * jnp.exp(cum_prev), r_abs=r * jnp.exp(cum),
                        k_end=kd * e_end, b_end=bb * e_end)
            e_tot = jnp.exp(tot)
            for p in range(n_pair):
                op = {name: _stack_heads(x[:, sls[p]]) for name, x in full.items()}
                op.update(incl=incl, strict=strict, e_tot=e_tot[:, sls[p]], bi=bi, d=d, p=p)
                ops.append(op)

    ch = range(len(ops))
    bf = lambda name: [ops[i][name].astype(BF16) for i in ch]
    a_s, r_s, k_s, b_s, v_s = bf("a_hat"), bf("r_hat"), bf("k_til"), bf("b_til"), bf("v")
    ar_s = [jnp.concatenate([a_s[i], r_s[i]], axis=0) for i in ch]
    ar_b = [_mm(ar_s[i], b_s[i], _NT) for i in ch]
    ar_k = [_mm(ar_s[i], k_s[i], _NT) for i in ch]
    low = [ops[i]["strict"] * ar_b[i][:c2] for i in ch]
    m_rb = [(ops[i]["incl"] * ar_b[i][c2:]).astype(BF16) for i in ch]
    m_kk = [jnp.concatenate([ops[i]["strict"] * ar_k[i][:c2], ops[i]["incl"] * ar_k[i][c2:]],
                            axis=0).astype(BF16) for i in ch]
    t_inv = None
    s = 1
    while s < CHUNK:
        join = jnp.logical_and(blk_r // (2 * s) == blk_c // (2 * s), blk_r // s != blk_c // s)
        join = join.astype(F32)
        if t_inv is None:
            t_inv = [eye - join * low[i] for i in ch]
        else:
            d2 = [t_inv[i].astype(BF16) for i in ch]
            ed = [_mm((join * low[i]).astype(BF16), d2[i]).astype(BF16) for i in ch]
            t_inv = [t_inv[i] - _mm(d2[i], ed[i]) for i in ch]
        s *= 2
    mv = [_mm(m_kk[i], v_s[i]) for i in ch]
    au = [_mm(t_inv[i].astype(BF16),
              jnp.concatenate([ops[i]["a_abs"], mv[i][:c2]], axis=1).astype(BF16)).astype(BF16)
          for i in ch]
    qo = [jnp.concatenate([ops[i]["r_abs"], mv[i][c2:]], axis=1) - _mm(m_rb[i], au[i]) for i in ch]
    ba = [_mm(ops[i]["b_end"].astype(BF16), au[i], _TN) for i in ch]
    kv = [_mm(ops[i]["k_end"].astype(BF16), v_s[i], _TN) for i in ch]
    where = [(ops[i]["bi"], ops[i]["d"], ops[i]["p"]) for i in ch]
    st = [_split(st_ref[where[i]]) for i in ch]
    g2 = [_split(eye_l * ops[i]["e_tot"] - ba[i][:, :LANES]) for i in ch]
    prod = [_mm(jnp.concatenate([qo[i][:, :LANES].astype(BF16), g2[i][0], g2[i][1]], axis=0),
                st[i][0]) for i in ch]
    g_lo = [_mm(g2[i][0], st[i][1]) for i in ch]
    for i in ch:
        bi, d, p = where[i]
        o_s = prod[i][:c2] + qo[i][:, LANES:]
        o_ref = of_ref if d == 0 else ob_ref
        o_ref[bi, :, sls[p]] = o_s[:CHUNK] + o_s[CHUNK:]
    for i in ch:
        st_ref[where[i]] = ((g_lo[i] + prod[i][c2 + LANES:]) + prod[i][c2:c2 + LANES]
                            + (kv[i] - ba[i][:, LANES:]))


def _rwkv_scan(r, v, kk, lw, kd, bb, ctx_len):
    nb, tt, hw = r.shape
    n_chunk = tt // CHUNK
    n_ctx = ctx_len // CHUNK

    def back(c):
        return jnp.where(c < n_ctx, n_ctx - 1 - c, n_chunk + n_ctx - 1 - c)

    sb = SCAN_BATCH
    assert nb % sb == 0
    fwd = pl.BlockSpec((sb, CHUNK, hw), lambda b, c: (b, c, 0))
    bwd = pl.BlockSpec((sb, CHUNK, hw), lambda b, c: (b, back(c), 0))
    fwd_d = pl.BlockSpec((1, sb, CHUNK, hw), lambda b, c: (0, b, c, 0))
    bwd_d = pl.BlockSpec((1, sb, CHUNK, hw), lambda b, c: (1, b, back(c), 0))
    return pl.pallas_call(
        functools.partial(_scan_kernel, hw=hw),
        grid=(nb // sb, n_chunk),
        in_specs=[fwd, fwd, fwd, fwd_d, fwd_d, fwd_d, bwd, bwd, bwd, bwd_d, bwd_d, bwd_d],
        out_specs=[fwd, bwd],
        out_shape=[jax.ShapeDtypeStruct((nb, tt, hw), F32)] * 2,
        scratch_shapes=[pltpu.VMEM((sb, 2, hw // LANES, LANES, LANES), F32)],
        compiler_params=_params("parallel", "arbitrary"),
    )(r, v, kk, lw, kd, bb, r, v, kk, lw, kd, bb)


def _back_kernel(x_ref, mod_ref, gain_ref, attn_ref, of_ref, ob_ref, bon_ref, g_ref, lg_ref, lb_ref,
                 w_ref, wup_ref, wdown_ref, o_ref, *, hw):
    mod = mod_ref[0, 0]
    o = of_ref[0] + ob_ref[0]
    cols = []
    for col in range(hw // LANES):
        sl = slice(col * LANES, (col + 1) * LANES)
        oc = o[:, sl]
        dev = oc - _head_sum(oc) * (1.0 / HEAD_DIM)
        var = _head_sum(dev * dev) * (1.0 / HEAD_DIM)
        cols.append(dev * lax.rsqrt(var + LNX_EPS))
    on = jnp.concatenate(cols, axis=1) * lg_ref[...] + lb_ref[...]
    rw = ((on + bon_ref[0]) * g_ref[0]).astype(BF16)
    aw = attn_ref.shape[2]
    y = (jnp.dot(attn_ref[0], w_ref[0:aw, :], preferred_element_type=F32)
         + jnp.dot(rw, w_ref[aw:, :], preferred_element_type=F32))
    x = x_ref[0] + mod[5:6] * y
    o_ref[0] = _ffn_apply(x, mod, 2, gain_ref[2:3], wup_ref, wdown_ref, FFN_CHUNKS)


def _back(x, modtab, gains, attn, o_fwd, o_bwd, bon, g, lnx_gain, lnx_bias, w_out, wup, wdown,
          layer, first_tile):
    nb, tt, d = x.shape
    hw = bon.shape[2]
    aw = attn.shape[2]
    n_tile = tt // TM - first_tile
    rows = lambda w: _row_specs(w, first_tile)
    return pl.pallas_call(
        functools.partial(_back_kernel, hw=hw),
        grid=(nb, n_tile),
        in_specs=[rows(d), _mod_spec(d, first_tile), _const_spec((3, d)), rows(aw), rows(hw),
                  rows(hw), rows(hw), rows(hw), _const_spec((1, hw)), _const_spec((1, hw)),
                  _stack_spec(w_out, layer), _stack_spec(wup, layer, 1),
                  _stack_spec(wdown, layer, 1)],
        out_specs=_row_specs(d),
        out_shape=jax.ShapeDtypeStruct((nb, n_tile * TM, d), F32),
        compiler_params=_params("parallel", "parallel"),
    )(x, modtab, gains, attn, o_fwd, o_bwd, bon, g, lnx_gain, lnx_bias, w_out, wup, wdown)


def kernel(x, c, ctx, c_ctx, w_mod, b_mod, norm_gain, ffn_up, ffn_down, w_in, q_gain, k_gain,
           na_bias, shift_mu, decay_w0, decay_w2, iclr_a0, iclr_a2, gate_g2, key_k, key_a,
           bonus_u, lnx_gain, lnx_bias, w_out):
    nb, seq, d = x.shape
    ctx_len = ctx.shape[1]
    depth = w_mod.shape[0]
    n_heads = na_bias.shape[1]
    attn_w = n_heads * HEAD_DIM
    hw = decay_w0.shape[2]
    lora_w = decay_w2.shape[2]
    assert ctx_len == TM and seq % (Q_ROWS * GRID_W) == 0 and attn_w % LANES == 0 and hw % LANES == 0
    assert lora_w + iclr_a2.shape[2] == LANES and gate_g2.shape[1] == LANES and nb + 1 <= 8

    cvec = jnp.zeros((8, d), F32).at[:nb].set(c).at[nb].set(c_ctx)
    mods = _modulation(cvec, w_mod, b_mod).reshape(depth, 8, 9, d)
    modtab = jnp.stack([jnp.broadcast_to(mods[:, nb:nb + 1], (depth, nb, 9, d)), mods[:, :nb]], axis=2)

    up_b, down_b = ffn_up.astype(BF16), ffn_down.astype(BF16)
    w_in_b, w_out_b = w_in.astype(BF16), w_out.astype(BF16)
    zpad = jnp.zeros_like(decay_w2)
    w2p = jnp.concatenate([decay_w2, zpad], axis=2)
    a2p = jnp.concatenate([zpad, iclr_a2], axis=2)
    bias_tabs = _bias_tables(na_bias)

    xs = jnp.concatenate([ctx, x], axis=1)
    for l in range(depth):
        mt = modtab[l]
        xs, q, k, v, pr = _front(xs, mt, norm_gain[l], up_b, down_b, w_in_b, l,
                                 jnp.tile(q_gain[l], 2)[None], jnp.tile(k_gain[l], 2)[None], attn_w)
        attn = _attention(q, k, v, bias_tabs[l], ctx_len)
        r, vr, kk, g, bon, lw, kd, bb = _rwkv_prepare(
            pr, shift_mu[l][None], decay_w0[l], w2p[l], iclr_a0[l], a2p[l], gate_g2[l],
            key_k[l][None], key_a[l][None], bonus_u[l].reshape(1, hw), hw)
        o_fwd, o_bwd = _rwkv_scan(r, vr, kk, lw, kd, bb, ctx_len)
        xs = _back(xs, mt, norm_gain[l], attn, o_fwd, o_bwd, bon, g, lnx_gain[l][None],
                   lnx_bias[l][None], w_out_b, up_b, down_b, l,
                   first_tile=ctx_len // TM if l == depth - 1 else 0)
    return xs
```

```python
import functools

import jax
import jax.numpy as jnp
import numpy as np
from jax import lax
from jax.experimental import pallas as pl
from jax.experimental.pallas import tpu as pltpu

F32 = jnp.float32
BF16 = jnp.bfloat16
HIGHEST = lax.Precision.HIGHEST

GRID_W = 64
NA_ROWS = 8
NA_COLS = 16
HEAD_DIM = 64
LANES = 128
TM = 256
CHUNK = 64
SCAN_BATCH = 4
Q_ROWS = 8
KEY_ROWS = Q_ROWS + NA_ROWS - 1
RMS_EPS = 1e-6
LNX_EPS = 64e-5
L2_EPS = 1e-24
NEG_INF = -1e30
ATTN_SCALE = HEAD_DIM ** -0.5
VMEM_LIMIT = 56 * 1024 * 1024


def _params(*sem):
    return pltpu.CompilerParams(dimension_semantics=sem, vmem_limit_bytes=VMEM_LIMIT)


def _bdot(a, b):
    return jnp.dot(a.astype(BF16), b.astype(BF16), preferred_element_type=F32)


_NN = (((1,), (0,)), ((), ()))
_NT = (((1,), (1,)), ((), ()))
_TN = (((0,), (0,)), ((), ()))


def _mm(a, b, dims=_NN):
    return lax.dot_general(a, b, dims, preferred_element_type=F32)


def _split(a):
    hi = a.astype(BF16)
    return hi, (a - hi.astype(F32)).astype(BF16)


def _mm3(a2, b2):
    (ah, al), (bh, bl) = a2, b2
    return (_mm(ah, bl) + _mm(al, bh)) + _mm(ah, bh)


def _sigmoid(x):
    return 1.0 / (1.0 + jnp.exp(-x))


def _head0_mask(shape):
    return lax.broadcasted_iota(jnp.int32, shape, len(shape) - 1) < HEAD_DIM


def _head_sum(z):
    m0 = _head0_mask(z.shape)
    s0 = jnp.sum(jnp.where(m0, z, 0.0), axis=-1, keepdims=True)
    s1 = jnp.sum(jnp.where(m0, 0.0, z), axis=-1, keepdims=True)
    return jnp.where(m0, s0, s1)


def _norm_modulate(x, gain, shift, scale):
    y = x * lax.rsqrt(jnp.mean(x * x, axis=-1, keepdims=True) + RMS_EPS)
    return (y * gain) * (1.0 + scale) + shift


def _mod_kernel(c_ref, w_ref, b_ref, o_ref):
    c = c_ref[...]
    o_ref[0] = _bdot(c * _sigmoid(c), w_ref[0]) + b_ref[0]


def _modulation(cvec, w_mod, b_mod):
    n_layer, d, nd = w_mod.shape
    return pl.pallas_call(
        _mod_kernel,
        grid=(n_layer, nd // d),
        in_specs=[pl.BlockSpec((8, d), lambda l, j: (0, 0)),
                  pl.BlockSpec((1, d, d), lambda l, j: (l, 0, j)),
                  pl.BlockSpec((1, 1, d), lambda l, j: (l, 0, j))],
        out_specs=pl.BlockSpec((1, 8, d), lambda l, j: (l, 0, j)),
        out_shape=jax.ShapeDtypeStruct((n_layer, 8, nd), F32),
        compiler_params=_params("parallel", "parallel"),
    )(cvec, w_mod, b_mod.reshape(n_layer, 1, nd))


def _ffn_apply(x, mod, sub, gain, wup_ref, wdown_ref, n_chunk):
    d_ff = wdown_ref.shape[0]
    ck = d_ff // n_chunk
    xn = _norm_modulate(x, gain, mod[3 * sub:3 * sub + 1], mod[3 * sub + 1:3 * sub + 2]).astype(BF16)
    acc = jnp.zeros(x.shape, F32)
    for c in range(n_chunk):
        gate = jnp.dot(xn, wup_ref[:, c * ck:(c + 1) * ck], preferred_element_type=F32)
        up = jnp.dot(xn, wup_ref[:, d_ff + c * ck:d_ff + (c + 1) * ck], preferred_element_type=F32)
        h = (gate * _sigmoid(gate) * up).astype(BF16)
        acc = acc + jnp.dot(h, wdown_ref[c * ck:(c + 1) * ck, :], preferred_element_type=F32)
    return x + (0.5 * mod[3 * sub + 2:3 * sub + 3]) * acc


FFN_CHUNKS = 1


def _row_specs(d, first=0):
    return pl.BlockSpec((1, TM, d), lambda b, t: (b, t + first, 0))


def _mod_spec(d, first=0):
    return pl.BlockSpec((1, 1, 9, d), lambda b, t: (b, jnp.minimum(t + first, 1), 0, 0))


def _const_spec(shape):
    return pl.BlockSpec(shape, lambda b, t: (0,) * len(shape), pipeline_mode=pl.Buffered(1))


def _stack_spec(stack, *lead):
    tail = stack.shape[len(lead):]
    return pl.BlockSpec((None,) * len(lead) + tail, lambda b, t: lead + (0,) * len(tail),
                        pipeline_mode=pl.Buffered(1))


def _front_kernel(x_ref, mod_ref, gain_ref, wup_ref, wdown_ref, w_ref, qg_ref, kg_ref,
                  xo_ref, q_ref, k_ref, v_ref, pr_ref, *, attn_w):
    mod = mod_ref[0, 0]
    x = _ffn_apply(x_ref[0], mod, 0, gain_ref[0:1], wup_ref, wdown_ref, FFN_CHUNKS)
    xo_ref[0] = x
    xn = _norm_modulate(x, gain_ref[1:2], mod[3:4], mod[4:5]).astype(BF16)
    p = jnp.dot(xn, w_ref[...], preferred_element_type=F32)
    for col in range(attn_w // LANES):
        sl = slice(col * LANES, (col + 1) * LANES)
        qc = p[:, sl]
        kc = p[:, attn_w + col * LANES:attn_w + (col + 1) * LANES]
        qn = qc * lax.rsqrt(_head_sum(qc * qc) * (1.0 / HEAD_DIM) + RMS_EPS) * qg_ref[...]
        kn = kc * lax.rsqrt(_head_sum(kc * kc) * (1.0 / HEAD_DIM) + RMS_EPS) * kg_ref[...]
        q_ref[0, :, sl] = (qn * ATTN_SCALE).astype(BF16)
        k_ref[0, :, sl] = kn.astype(BF16)
    v_ref[0] = p[:, 2 * attn_w:3 * attn_w].astype(BF16)
    pr_ref[0] = p[:, 3 * attn_w:]


def _front(x, modtab, gains, wup, wdown, w_in, layer, q_gain2, k_gain2, attn_w):
    nb, tt, d = x.shape
    d_in = w_in.shape[-1]
    rw = d_in - 3 * attn_w
    return pl.pallas_call(
        functools.partial(_front_kernel, attn_w=attn_w),
        grid=(nb, tt // TM),
        in_specs=[_row_specs(d), _mod_spec(d), _const_spec((3, d)),
                  _stack_spec(wup, layer, 0), _stack_spec(wdown, layer, 0),
                  _stack_spec(w_in, layer),
                  _const_spec((1, LANES)), _const_spec((1, LANES))],
        out_specs=[_row_specs(d), _row_specs(attn_w), _row_specs(attn_w), _row_specs(attn_w),
                   _row_specs(rw)],
        out_shape=[jax.ShapeDtypeStruct(x.shape, F32)]
        + [jax.ShapeDtypeStruct((nb, tt, attn_w), BF16)] * 3
        + [jax.ShapeDtypeStruct((nb, tt, rw), F32)],
        compiler_params=_params("parallel", "parallel"),
    )(x, modtab, gains, wup, wdown, w_in, q_gain2, k_gain2)


_TAB_PAD = 4 * LANES
_TAB_W = 16 * LANES


def _bias_layout(n_blk):
    rows = n_blk * Q_ROWS
    pats = []
    for blk in range(n_blk):
        key_row0 = int(np.clip(Q_ROWS * blk - NA_ROWS // 2, 0, rows - KEY_ROWS))
        pat = []
        for m in range(Q_ROWS):
            qi = Q_ROWS * blk + m
            start = int(np.clip(qi - NA_ROWS // 2, 0, rows - NA_ROWS))
            dst = (start - key_row0) * GRID_W
            off = _TAB_PAD + (start - qi + NA_ROWS - 1) * GRID_W - dst
            copy = (off % LANES) // HEAD_DIM
            base = off + copy * HEAD_DIM
            assert base >= 0 and base % LANES == 0 and base + KEY_ROWS * GRID_W <= _TAB_W
            pat.append((copy, base, dst))
        pats.append(tuple(pat))
    assert all(p == pats[1] for p in pats[1:n_blk - 1])
    return pats[0], pats[1], pats[n_blk - 1]


def _attn_kernel(q_ref, k_ref, v_ref, tab_ref, o_ref, bias_ref, *, ctx_len, n_blk):
    j = pl.program_id(2)
    k_ctx = k_ref[0, 0:ctx_len, :]
    v_ctx = v_ref[0, 0:ctx_len, :]
    nt = (((1,), (1,)), ((), ()))
    key_w = KEY_ROWS * GRID_W
    q_rows = Q_ROWS * GRID_W
    nc = range(2)

    def head_qs(q):
        m0 = _head0_mask(q.shape)
        zero = jnp.zeros_like(q)
        return [jnp.where(m0, q, zero), jnp.where(m0, zero, q)]

    def merge(outs):
        return jnp.where(_head0_mask(outs[0].shape), outs[0], outs[1]).astype(BF16)

    lane = lax.broadcasted_iota(jnp.int32, (GRID_W, key_w), 1)
    for first_j, pat in zip((1, 2, n_blk), _bias_layout(n_blk)):
        @pl.when(j == first_j)
        def _(pat=pat):
            for hh in range(2):
                for m, (copy, base, dst) in enumerate(pat):
                    strip = tab_ref[hh, copy, :, base:base + key_w]
                    ok = jnp.logical_and(lane >= dst, lane < dst + NA_ROWS * GRID_W)
                    bias_ref[hh, m * GRID_W:(m + 1) * GRID_W, :] = jnp.where(ok, strip, NEG_INF)

    @pl.when(j == 0)
    def _():
        qs = head_qs(q_ref[0, 0:ctx_len, :])
        s = [lax.dot_general(qs[i], k_ctx, nt, preferred_element_type=F32) for i in nc]
        p = [jnp.exp(s[i] - jnp.max(s[i], axis=-1, keepdims=True)) for i in nc]
        o = [jnp.dot(p[i].astype(BF16), v_ctx, preferred_element_type=F32) for i in nc]
        o_ref[0, 0:ctx_len, :] = merge(
            [o[i] * (1.0 / jnp.sum(p[i], axis=-1, keepdims=True)) for i in nc])

    @pl.when(j > 0)
    def _():
        q_row0 = pl.multiple_of(ctx_len + (j - 1) * q_rows, GRID_W)
        key_row0 = jnp.clip(Q_ROWS * (j - 1) - NA_ROWS // 2, 0, n_blk * Q_ROWS - KEY_ROWS)
        row0 = pl.multiple_of(ctx_len + key_row0 * GRID_W, GRID_W)
        qs = head_qs(q_ref[0, pl.ds(q_row0, q_rows), :])
        k_loc = k_ref[0, pl.ds(row0, key_w), :]
        v_loc = v_ref[0, pl.ds(row0, key_w), :]
        s_loc = [lax.dot_general(qs[i], k_loc, nt, preferred_element_type=F32) + bias_ref[i]
                 for i in nc]
        s_ctx = [lax.dot_general(qs[i], k_ctx, nt, preferred_element_type=F32) for i in nc]
        mx = [jnp.maximum(jnp.max(s_loc[i], axis=-1, keepdims=True),
                          jnp.max(s_ctx[i], axis=-1, keepdims=True)) for i in nc]
        p_loc = [jnp.exp(s_loc[i] - mx[i]) for i in nc]
        p_ctx = [jnp.exp(s_ctx[i] - mx[i]) for i in nc]
        den = [jnp.sum(p_loc[i], axis=-1, keepdims=True) + jnp.sum(p_ctx[i], axis=-1, keepdims=True)
               for i in nc]
        o = [jnp.dot(p_loc[i].astype(BF16), v_loc, preferred_element_type=F32)
             + jnp.dot(p_ctx[i].astype(BF16), v_ctx, preferred_element_type=F32) for i in nc]
        o_ref[0, pl.ds(q_row0, q_rows), :] = merge([o[i] * (1.0 / den[i]) for i in nc])


def _attention(q, k, v, table, ctx_len):
    nb, tt, w = q.shape
    q_rows = Q_ROWS * GRID_W
    n_blk = (tt - ctx_len) // q_rows
    assert (tt - ctx_len) % q_rows == 0 and ctx_len % GRID_W == 0 and n_blk >= 3
    whole = pl.BlockSpec((1, tt, LANES), lambda hp, b, j: (b, 0, hp))
    return pl.pallas_call(
        functools.partial(_attn_kernel, ctx_len=ctx_len, n_blk=n_blk),
        grid=(w // LANES, nb, n_blk + 1),
        in_specs=[whole, whole, whole,
                  pl.BlockSpec((2, 2, GRID_W, _TAB_W), lambda hp, b, j: (hp, 0, 0, 0))],
        out_specs=whole,
        out_shape=jax.ShapeDtypeStruct((nb, tt, w), BF16),
        scratch_shapes=[pltpu.VMEM((2, q_rows, KEY_ROWS * GRID_W), F32)],
        compiler_params=_params("parallel", "parallel", "arbitrary"),
    )(q, k, v, table)


def _bias_tables(na_bias):
    n_layer, n_head, n_dr, n_dc = na_bias.shape
    cols = np.arange(GRID_W)
    col_start = np.clip(cols - NA_COLS // 2, 0, GRID_W - NA_COLS)
    in_window = (cols[None, :] >= col_start[:, None]) & (cols[None, :] < col_start[:, None] + NA_COLS)
    dc = np.clip(cols[None, :] - cols[:, None], -(NA_COLS - 1), NA_COLS - 1) + (NA_COLS - 1)
    onehot = (dc[None] == np.arange(n_dc)[:, None, None]).astype(np.float32)
    tiles = jnp.einsum('lhrj,jqk->lhqrk', na_bias.astype(F32), onehot, precision=HIGHEST)
    tiles = jnp.where(in_window[None, None, :, None, :], tiles, NEG_INF)
    strip = tiles.reshape(n_layer, n_head, GRID_W, n_dr * GRID_W)
    copies = []
    for shift in (0, HEAD_DIM):
        left = _TAB_PAD + shift
        copies.append(jnp.pad(strip, ((0, 0), (0, 0), (0, 0), (left, _TAB_W - left - n_dr * GRID_W)),
                              constant_values=NEG_INF))
    return jnp.stack(copies, axis=2)


def _prep_kernel(p_ref, hp_ref, hn_ref, mu_ref, w0_ref, w2_ref, a0_ref, a2_ref, g2_ref,
                 kk_ref, ka_ref, u_ref,
                 r_ref, v_ref, kkn_ref, g_ref, bon_ref, lw_ref, kd_ref, bb_ref, *, hw, n_tile):
    t = pl.program_id(1)
    p = p_ref[0]
    rows = lax.broadcasted_iota(jnp.int32, (8, 1), 0)
    prev_ok = (t >= 2).astype(F32)
    next_ok = jnp.logical_and(t >= 1, t <= n_tile - 2).astype(F32)
    prev = pltpu.roll(p, 1, 0)
    nxt = pltpu.roll(p, TM - 1, 0)
    prev = jnp.concatenate(
        [jnp.where(rows == 0, hp_ref[0, 7:8, :] * prev_ok, prev[0:8]), prev[8:]], axis=0)
    nxt = jnp.concatenate(
        [nxt[:TM - 8], jnp.where(rows == 7, hn_ref[0, 0:1, :] * next_ok, nxt[TM - 8:])], axis=0)
    mu = mu_ref[...]
    p = p * (1.0 - mu) + (0.5 * mu) * (prev + nxt)
    r = p[:, 0:hw]
    k = p[:, hw:2 * hw]
    v = p[:, 2 * hw:3 * hw]
    lora = p[:, 3 * hw:3 * hw + LANES]
    dg = p[:, 3 * hw + LANES:3 * hw + 2 * LANES]
    g_ref[0] = _bdot(_sigmoid(dg), g2_ref[...])
    r_ref[0] = r
    v_ref[0] = v
    tl = jnp.tanh(lora)
    ksum = jnp.zeros_like(k)
    kks = []
    for col in range(hw // LANES):
        sl = slice(col * LANES, (col + 1) * LANES)
        kc = k[:, sl] * kk_ref[:, sl]
        kks.append(kc * lax.rsqrt(jnp.maximum(_head_sum(kc * kc), L2_EPS)))
    kkn = jnp.concatenate(kks, axis=1)
    kkn_ref[0] = kkn
    for d in range(2):
        z = w0_ref[d:d + 1, :] + _bdot(tl, w2_ref[d])
        lw_ref[d, 0] = -np.float32(np.exp(-0.5)) * _sigmoid(z)
        a = _sigmoid(a0_ref[d:d + 1, :] + _bdot(lora, a2_ref[d]))
        kd = k * (1.0 + (a - 1.0) * ka_ref[...])
        kd_ref[d, 0] = kd
        bb_ref[d, 0] = kkn * a
        ksum = ksum + kd
    bon = r * ksum * u_ref[...]
    for col in range(hw // LANES):
        sl = slice(col * LANES, (col + 1) * LANES)
        bon_ref[0, :, sl] = _head_sum(bon[:, sl]) * v[:, sl]


def _rwkv_prepare(pr, shift_mu, w0, w2p, a0, a2p, g2, key_k, key_a, bonus_u, hw):
    nb, tt, rw = pr.shape
    n_tile = tt // TM
    sub = TM // 8
    tok = jax.ShapeDtypeStruct((nb, tt, hw), F32)
    tok2 = jax.ShapeDtypeStruct((2, nb, tt, hw), F32)
    tok_spec = _row_specs(hw)
    tok2_spec = pl.BlockSpec((2, 1, TM, hw), lambda b, t: (0, b, t, 0))
    return pl.pallas_call(
        functools.partial(_prep_kernel, hw=hw, n_tile=n_tile),
        grid=(nb, n_tile),
        in_specs=[_row_specs(rw),
                  pl.BlockSpec((1, 8, rw), lambda b, t: (b, jnp.maximum(t * sub - 1, 0), 0)),
                  pl.BlockSpec((1, 8, rw), lambda b, t: (b, jnp.minimum((t + 1) * sub, tt // 8 - 1), 0)),
                  _const_spec((1, rw)), _const_spec((2, hw)), _const_spec((2, LANES, hw)),
                  _const_spec((2, hw)), _const_spec((2, LANES, hw)), _const_spec((LANES, hw)),
                  _const_spec((1, hw)), _const_spec((1, hw)), _const_spec((1, hw))],
        out_specs=[tok_spec] * 5 + [tok2_spec] * 3,
        out_shape=[tok] * 5 + [tok2] * 3,
        compiler_params=_params("parallel", "parallel"),
    )(pr, pr, pr, shift_mu, w0, w2p, a0, a2p, g2, key_k, key_a, bonus_u)


def _stack_heads(x):
    m0 = _head0_mask(x.shape)
    zero = jnp.zeros_like(x)
    return jnp.concatenate([jnp.where(m0, x, zero), jnp.where(m0, zero, x)], axis=0)


def _scan_kernel(rf_ref, vf_ref, kkf_ref, lwf_ref, kdf_ref, bbf_ref,
                 rb_ref, vb_ref, kkb_ref, lwb_ref, kdb_ref, bbb_ref,
                 of_ref, ob_ref, st_ref, *, hw):
    n_pair = hw // LANES
    c2 = 2 * CHUNK

    @pl.when(pl.program_id(1) == 0)
    def _():
        st_ref[...] = jnp.zeros(st_ref.shape, F32)

    ti = lax.broadcasted_iota(jnp.int32, (CHUNK, CHUNK), 0)
    tj = lax.broadcasted_iota(jnp.int32, (CHUNK, CHUNK), 1)
    ri = lax.broadcasted_iota(jnp.int32, (c2, c2), 0)
    rj = lax.broadcasted_iota(jnp.int32, (c2, c2), 1)
    same = (ri // CHUNK) == (rj // CHUNK)
    blk_r, blk_c = ri % CHUNK, rj % CHUNK
    eye = (ri == rj).astype(F32)
    li = lax.broadcasted_iota(jnp.int32, (LANES, LANES), 0)
    lj = lax.broadcasted_iota(jnp.int32, (LANES, LANES), 1)
    eye_l = (li == lj).astype(F32)
    sls = [slice(pr * LANES, (pr + 1) * LANES) for pr in range(n_pair)]

    ops = []
    for bi in range(rf_ref.shape[0]):
        for d, (r_ref, v_ref, kk_ref, lw_ref, kd_ref, bb_ref) in enumerate(
                ((rf_ref, vf_ref, kkf_ref, lwf_ref, kdf_ref, bbf_ref),
                 (rb_ref, vb_ref, kkb_ref, lwb_ref, kdb_ref, bbb_ref))):
            incl1 = (tj <= ti) if d == 0 else (tj >= ti)
            before = (blk_c - blk_r) if d == 0 else (blk_r - blk_c)
            incl = jnp.logical_and(same, before <= 0).astype(F32)
            strict = jnp.logical_and(same, before < 0).astype(F32)
            lw = lw_ref[0, bi]
            incl1 = incl1.astype(F32).astype(BF16)
            lw1 = lw.astype(BF16)
            lw2, lw3 = _split(lw - lw1.astype(F32))
            cum = (_mm(incl1, lw3) + _mm(incl1, lw2)) + _mm(incl1, lw1)
            cum_prev = cum - lw
            tot = jnp.sum(lw, axis=0, keepdims=True)
            mid = 0.5 * tot
            r, v, kk = r_ref[bi], v_ref[bi], kk_ref[bi]
            kd, bb = kd_ref[0, bi], bb_ref[0, bi]
            e_inv = jnp.exp(mid - cum)
            e_end = jnp.exp(tot - cum)
            full = dict(a_hat=kk * jnp.exp(cum_prev - mid), r_hat=r * jnp.exp(cum - mid),
                        k_til=kd * e_inv, b_til=bb * e_inv, v=v,
                        a_abs=kk * jnp.exp(cum_prev), r_abs=r * jnp.exp(cum),
                        k_end=kd * e_end, b_end=bb * e_end)
            e_tot = jnp.exp(tot)
            for p in range(n_pair):
                op = {name: _stack_heads(x[:, sls[p]]) for name, x in full.items()}
                op.update(incl=incl, strict=strict, e_tot=e_tot[:, sls[p]], bi=bi, d=d, p=p)
                ops.append(op)

    ch = range(len(ops))
    bf = lambda name: [ops[i][name].astype(BF16) for i in ch]
    a_s, r_s, k_s, b_s, v_s = bf("a_hat"), bf("r_hat"), bf("k_til"), bf("b_til"), bf("v")
    ar_s = [jnp.concatenate([a_s[i], r_s[i]], axis=0) for i in ch]
    ar_b = [_mm(ar_s[i], b_s[i], _NT) for i in ch]
    ar_k = [_mm(ar_s[i], k_s[i], _NT) for i in ch]
    low = [ops[i]["strict"] * ar_b[i][:c2] for i in ch]
    m_rb = [(ops[i]["incl"] * ar_b[i][c2:]).astype(BF16) for i in ch]
    m_kk = [jnp.concatenate([ops[i]["strict"] * ar_k[i][:c2], ops[i]["incl"] * ar_k[i][c2:]],
                            axis=0).astype(BF16) for i in ch]
    t_inv = None
    s = 1
    while s < CHUNK:
        join = jnp.logical_and(blk_r // (2 * s) == blk_c // (2 * s), blk_r // s != blk_c // s)
        join = join.astype(F32)
        if t_inv is None:
            t_inv = [eye - join * low[i] for i in ch]
        else:
            d2 = [t_inv[i].astype(BF16) for i in ch]
            ed = [_mm((join * low[i]).astype(BF16), d2[i]).astype(BF16) for i in ch]
            t_inv = [t_inv[i] - _mm(d2[i], ed[i]) for i in ch]
        s *= 2
    mv = [_mm(m_kk[i], v_s[i]) for i in ch]
    au = [_mm(t_inv[i].astype(BF16),
              jnp.concatenate([ops[i]["a_abs"], mv[i][:c2]], axis=1).astype(BF16)).astype(BF16)
          for i in ch]
    qo = [jnp.concatenate([ops[i]["r_abs"], mv[i][c2:]], axis=1) - _mm(m_rb[i], au[i]) for i in ch]
    ba = [_mm(ops[i]["b_end"].astype(BF16), au[i], _TN) for i in ch]
    kv = [_mm(ops[i]["k_end"].astype(BF16), v_s[i], _TN) for i in ch]
    where = [(ops[i]["bi"], ops[i]["d"], ops[i]["p"]) for i in ch]
    st = [_split(st_ref[where[i]]) for i in ch]
    g2 = [_split(eye_l * ops[i]["e_tot"] - ba[i][:, :LANES]) for i in ch]
    prod = [_mm(jnp.concatenate([qo[i][:, :LANES].astype(BF16), g2[i][0], g2[i][1]], axis=0),
                st[i][0]) for i in ch]
    g_lo = [_mm(g2[i][0], st[i][1]) for i in ch]
    for i in ch:
        bi, d, p = where[i]
        o_s = prod[i][:c2] + qo[i][:, LANES:]
        o_ref = of_ref if d == 0 else ob_ref
        o_ref[bi, :, sls[p]] = o_s[:CHUNK] + o_s[CHUNK:]
    for i in ch:
        st_ref[where[i]] = ((g_lo[i] + prod[i][c2 + LANES:]) + prod[i][c2:c2 + LANES]
                            + (kv[i] - ba[i][:, LANES:]))


def _rwkv_scan(r, v, kk, lw, kd, bb, ctx_len):
    nb, tt, hw = r.shape
    n_chunk = tt // CHUNK
    n_ctx = ctx_len // CHUNK

    def back(c):
        return jnp.where(c < n_ctx, n_ctx - 1 - c, n_chunk + n_ctx - 1 - c)

    sb = SCAN_BATCH
    assert nb % sb == 0
    fwd = pl.BlockSpec((sb, CHUNK, hw), lambda b, c: (b, c, 0))
    bwd = pl.BlockSpec((sb, CHUNK, hw), lambda b, c: (b, back(c), 0))
    fwd_d = pl.BlockSpec((1, sb, CHUNK, hw), lambda b, c: (0, b, c, 0))
    bwd_d = pl.BlockSpec((1, sb, CHUNK, hw), lambda b, c: (1, b, back(c), 0))
    return pl.pallas_call(
        functools.partial(_scan_kernel, hw=hw),
        grid=(nb // sb, n_chunk),
        in_specs=[fwd, fwd, fwd, fwd_d, fwd_d, fwd_d, bwd, bwd, bwd, bwd_d, bwd_d, bwd_d],
        out_specs=[fwd, bwd],
        out_shape=[jax.ShapeDtypeStruct((nb, tt, hw), F32)] * 2,
        scratch_shapes=[pltpu.VMEM((sb, 2, hw // LANES, LANES, LANES), F32)],
        compiler_params=_params("parallel", "arbitrary"),
    )(r, v, kk, lw, kd, bb, r, v, kk, lw, kd, bb)


def _back_kernel(x_ref, mod_ref, gain_ref, attn_ref, of_ref, ob_ref, bon_ref, g_ref, lg_ref, lb_ref,
                 w_ref, wup_ref, wdown_ref, o_ref, *, hw):
    mod = mod_ref[0, 0]
    o = of_ref[0] + ob_ref[0]
    cols = []
    for col in range(hw // LANES):
        sl = slice(col * LANES, (col + 1) * LANES)
        oc = o[:, sl]
        dev = oc - _head_sum(oc) * (1.0 / HEAD_DIM)
        var = _head_sum(dev * dev) * (1.0 / HEAD_DIM)
        cols.append(dev * lax.rsqrt(var + LNX_EPS))
    on = jnp.concatenate(cols, axis=1) * lg_ref[...] + lb_ref[...]
    rw = ((on + bon_ref[0]) * g_ref[0]).astype(BF16)
    aw = attn_ref.shape[2]
    y = (jnp.dot(attn_ref[0], w_ref[0:aw, :], preferred_element_type=F32)
         + jnp.dot(rw, w_ref[aw:, :], preferred_element_type=F32))
    x = x_ref[0] + mod[5:6] * y
    o_ref[0] = _ffn_apply(x, mod, 2, gain_ref[2:3], wup_ref, wdown_ref, FFN_CHUNKS)


def _back(x, modtab, gains, attn, o_fwd, o_bwd, bon, g, lnx_gain, lnx_bias, w_out, wup, wdown,
          layer, first_tile):
    nb, tt, d = x.shape
    hw = bon.shape[2]
    aw = attn.shape[2]
    n_tile = tt // TM - first_tile
    rows = lambda w: _row_specs(w, first_tile)
    return pl.pallas_call(
        functools.partial(_back_kernel, hw=hw),
        grid=(nb, n_tile),
        in_specs=[rows(d), _mod_spec(d, first_tile), _const_spec((3, d)), rows(aw), rows(hw),
                  rows(hw), rows(hw), rows(hw), _const_spec((1, hw)), _const_spec((1, hw)),
                  _stack_spec(w_out, layer), _stack_spec(wup, layer, 1),
                  _stack_spec(wdown, layer, 1)],
        out_specs=_row_specs(d),
        out_shape=jax.ShapeDtypeStruct((nb, n_tile * TM, d), F32),
        compiler_params=_params("parallel", "parallel"),
    )(x, modtab, gains, attn, o_fwd, o_bwd, bon, g, lnx_gain, lnx_bias, w_out, wup, wdown)


def kernel(x, c, ctx, c_ctx, w_mod, b_mod, norm_gain, ffn_up, ffn_down, w_in, q_gain, k_gain,
           na_bias, shift_mu, decay_w0, decay_w2, iclr_a0, iclr_a2, gate_g2, key_k, key_a,
           bonus_u, lnx_gain, lnx_bias, w_out):
    nb, seq, d = x.shape
    ctx_len = ctx.shape[1]
    depth = w_mod.shape[0]
    n_heads = na_bias.shape[1]
    attn_w = n_heads * HEAD_DIM
    hw = decay_w0.shape[2]
    lora_w = decay_w2.shape[2]
    assert ctx_len == TM and seq % (Q_ROWS * GRID_W) == 0 and attn_w % LANES == 0 and hw % LANES == 0
    assert lora_w + iclr_a2.shape[2] == LANES and gate_g2.shape[1] == LANES and nb + 1 <= 8

    cvec = jnp.zeros((8, d), F32).at[:nb].set(c).at[nb].set(c_ctx)
    mods = _modulation(cvec, w_mod, b_mod).reshape(depth, 8, 9, d)
    modtab = jnp.stack([jnp.broadcast_to(mods[:, nb:nb + 1], (depth, nb, 9, d)), mods[:, :nb]], axis=2)

    up_b, down_b = ffn_up.astype(BF16), ffn_down.astype(BF16)
    w_in_b, w_out_b = w_in.astype(BF16), w_out.astype(BF16)
    zpad = jnp.zeros_like(decay_w2)
    w2p = jnp.concatenate([decay_w2, zpad], axis=2)
    a2p = jnp.concatenate([zpad, iclr_a2], axis=2)
    bias_tabs = _bias_tables(na_bias)

    xs = jnp.concatenate([ctx, x], axis=1)
    for l in range(depth):
        mt = modtab[l]
        xs, q, k, v, pr = _front(xs, mt, norm_gain[l], up_b, down_b, w_in_b, l,
                                 jnp.tile(q_gain[l], 2)[None], jnp.tile(k_gain[l], 2)[None], attn_w)
        attn = _attention(q, k, v, bias_tabs[l], ctx_len)
        r, vr, kk, g, bon, lw, kd, bb = _rwkv_prepare(
            pr, shift_mu[l][None], decay_w0[l], w2p[l], iclr_a0[l], a2p[l], gate_g2[l],
            key_k[l][None], key_a[l][None], bonus_u[l].reshape(1, hw), hw)
        o_fwd, o_bwd = _rwkv_scan(r, vr, kk, lw, kd, bb, ctx_len)
        xs = _back(xs, mt, norm_gain[l], attn, o_fwd, o_bwd, bon, g, lnx_gain[l][None],
                   lnx_bias[l][None], w_out_b, up_b, down_b, l,
                   first_tile=ctx_len // TM if l == depth - 1 else 0)
    return xs
```

```python
import functools

import jax
import jax.numpy as jnp
import numpy as np
from jax import lax
from jax.experimental import pallas as pl
from jax.experimental.pallas import tpu as pltpu

F32 = jnp.float32
BF16 = jnp.bfloat16
HIGHEST = lax.Precision.HIGHEST

GRID_W = 64
NA_ROWS = 8
NA_COLS = 16
HEAD_DIM = 64
LANES = 128
TM = 256
CHUNK = 64
SCAN_BATCH = 2
Q_ROWS = 8
KEY_ROWS = Q_ROWS + NA_ROWS - 1
RMS_EPS = 1e-6
LNX_EPS = 64e-5
L2_EPS = 1e-24
NEG_INF = -1e30
ATTN_SCALE = HEAD_DIM ** -0.5
VMEM_LIMIT = 56 * 1024 * 1024


def _params(*sem):
    return pltpu.CompilerParams(dimension_semantics=sem, vmem_limit_bytes=VMEM_LIMIT)


def _bdot(a, b):
    return jnp.dot(a.astype(BF16), b.astype(BF16), preferred_element_type=F32)


_NN = (((1,), (0,)), ((), ()))
_NT = (((1,), (1,)), ((), ()))
_TN = (((0,), (0,)), ((), ()))


def _mm(a, b, dims=_NN):
    return lax.dot_general(a, b, dims, preferred_element_type=F32)


def _split(a):
    hi = a.astype(BF16)
    return hi, (a - hi.astype(F32)).astype(BF16)


def _mm3(a2, b2):
    (ah, al), (bh, bl) = a2, b2
    return (_mm(ah, bl) + _mm(al, bh)) + _mm(ah, bh)


def _sigmoid(x):
    return 1.0 / (1.0 + jnp.exp(-x))


def _head0_mask(shape):
    return lax.broadcasted_iota(jnp.int32, shape, len(shape) - 1) < HEAD_DIM


def _head_sum(z):
    m0 = _head0_mask(z.shape)
    s0 = jnp.sum(jnp.where(m0, z, 0.0), axis=-1, keepdims=True)
    s1 = jnp.sum(jnp.where(m0, 0.0, z), axis=-1, keepdims=True)
    return jnp.where(m0, s0, s1)


def _norm_modulate(x, gain, shift, scale):
    y = x * lax.rsqrt(jnp.mean(x * x, axis=-1, keepdims=True) + RMS_EPS)
    return (y * gain) * (1.0 + scale) + shift


def _mod_kernel(c_ref, w_ref, b_ref, o_ref):
    c = c_ref[...]
    o_ref[0] = _bdot(c * _sigmoid(c), w_ref[0]) + b_ref[0]


def _modulation(cvec, w_mod, b_mod):
    n_layer, d, nd = w_mod.shape
    return pl.pallas_call(
        _mod_kernel,
        grid=(n_layer, nd // d),
        in_specs=[pl.BlockSpec((8, d), lambda l, j: (0, 0)),
                  pl.BlockSpec((1, d, d), lambda l, j: (l, 0, j)),
                  pl.BlockSpec((1, 1, d), lambda l, j: (l, 0, j))],
        out_specs=pl.BlockSpec((1, 8, d), lambda l, j: (l, 0, j)),
        out_shape=jax.ShapeDtypeStruct((n_layer, 8, nd), F32),
        compiler_params=_params("parallel", "parallel"),
    )(cvec, w_mod, b_mod.reshape(n_layer, 1, nd))


def _ffn_apply(x, mod, sub, gain, wup_ref, wdown_ref, n_chunk):
    d_ff = wdown_ref.shape[0]
    ck = d_ff // n_chunk
    xn = _norm_modulate(x, gain, mod[3 * sub:3 * sub + 1], mod[3 * sub + 1:3 * sub + 2]).astype(BF16)
    acc = jnp.zeros(x.shape, F32)
    for c in range(n_chunk):
        gate = jnp.dot(xn, wup_ref[:, c * ck:(c + 1) * ck], preferred_element_type=F32)
        up = jnp.dot(xn, wup_ref[:, d_ff + c * ck:d_ff + (c + 1) * ck], preferred_element_type=F32)
        h = (gate * _sigmoid(gate) * up).astype(BF16)
        acc = acc + jnp.dot(h, wdown_ref[c * ck:(c + 1) * ck, :], preferred_element_type=F32)
    return x + (0.5 * mod[3 * sub + 2:3 * sub + 3]) * acc


FFN_CHUNKS = 1


def _row_specs(d, first=0):
    return pl.BlockSpec((1, TM, d), lambda b, t: (b, t + first, 0))


def _mod_spec(d, first=0):
    return pl.BlockSpec((1, 1, 9, d), lambda b, t: (b, jnp.minimum(t + first, 1), 0, 0))


def _const_spec(shape):
    return pl.BlockSpec(shape, lambda b, t: (0,) * len(shape), pipeline_mode=pl.Buffered(1))


def _stack_spec(stack, *lead):
    tail = stack.shape[len(lead):]
    return pl.BlockSpec((None,) * len(lead) + tail, lambda b, t: lead + (0,) * len(tail),
                        pipeline_mode=pl.Buffered(1))


def _front_kernel(x_ref, mod_ref, gain_ref, wup_ref, wdown_ref, w_ref, qg_ref, kg_ref,
                  xo_ref, q_ref, k_ref, v_ref, pr_ref, *, attn_w):
    mod = mod_ref[0, 0]
    x = _ffn_apply(x_ref[0], mod, 0, gain_ref[0:1], wup_ref, wdown_ref, FFN_CHUNKS)
    xo_ref[0] = x
    xn = _norm_modulate(x, gain_ref[1:2], mod[3:4], mod[4:5]).astype(BF16)
    p = jnp.dot(xn, w_ref[...], preferred_element_type=F32)
    for col in range(attn_w // LANES):
        sl = slice(col * LANES, (col + 1) * LANES)
        qc = p[:, sl]
        kc = p[:, attn_w + col * LANES:attn_w + (col + 1) * LANES]
        qn = qc * lax.rsqrt(_head_sum(qc * qc) * (1.0 / HEAD_DIM) + RMS_EPS) * qg_ref[...]
        kn = kc * lax.rsqrt(_head_sum(kc * kc) * (1.0 / HEAD_DIM) + RMS_EPS) * kg_ref[...]
        q_ref[0, :, sl] = (qn * ATTN_SCALE).astype(BF16)
        k_ref[0, :, sl] = kn.astype(BF16)
    v_ref[0] = p[:, 2 * attn_w:3 * attn_w].astype(BF16)
    pr_ref[0] = p[:, 3 * attn_w:]


def _front(x, modtab, gains, wup, wdown, w_in, layer, q_gain2, k_gain2, attn_w):
    nb, tt, d = x.shape
    d_in = w_in.shape[-1]
    rw = d_in - 3 * attn_w
    return pl.pallas_call(
        functools.partial(_front_kernel, attn_w=attn_w),
        grid=(nb, tt // TM),
        in_specs=[_row_specs(d), _mod_spec(d), _const_spec((3, d)),
                  _stack_spec(wup, layer, 0), _stack_spec(wdown, layer, 0),
                  _stack_spec(w_in, layer),
                  _const_spec((1, LANES)), _const_spec((1, LANES))],
        out_specs=[_row_specs(d), _row_specs(attn_w), _row_specs(attn_w), _row_specs(attn_w),
                   _row_specs(rw)],
        out_shape=[jax.ShapeDtypeStruct(x.shape, F32)]
        + [jax.ShapeDtypeStruct((nb, tt, attn_w), BF16)] * 3
        + [jax.ShapeDtypeStruct((nb, tt, rw), F32)],
        compiler_params=_params("parallel", "parallel"),
    )(x, modtab, gains, wup, wdown, w_in, q_gain2, k_gain2)


_TAB_PAD = 4 * LANES
_TAB_W = 16 * LANES


def _bias_layout(n_blk):
    rows = n_blk * Q_ROWS
    pats = []
    for blk in range(n_blk):
        key_row0 = int(np.clip(Q_ROWS * blk - NA_ROWS // 2, 0, rows - KEY_ROWS))
        pat = []
        for m in range(Q_ROWS):
            qi = Q_ROWS * blk + m
            start = int(np.clip(qi - NA_ROWS // 2, 0, rows - NA_ROWS))
            dst = (start - key_row0) * GRID_W
            off = _TAB_PAD + (start - qi + NA_ROWS - 1) * GRID_W - dst
            copy = (off % LANES) // HEAD_DIM
            base = off + copy * HEAD_DIM
            assert base >= 0 and base % LANES == 0 and base + KEY_ROWS * GRID_W <= _TAB_W
            pat.append((copy, base, dst))
        pats.append(tuple(pat))
    assert all(p == pats[1] for p in pats[1:n_blk - 1])
    return pats[0], pats[1], pats[n_blk - 1]


def _attn_kernel(q_ref, k_ref, v_ref, tab_ref, o_ref, bias_ref, *, ctx_len, n_blk):
    j = pl.program_id(2)
    k_ctx = k_ref[0, 0:ctx_len, :]
    v_ctx = v_ref[0, 0:ctx_len, :]
    nt = (((1,), (1,)), ((), ()))
    key_w = KEY_ROWS * GRID_W
    q_rows = Q_ROWS * GRID_W
    nc = range(2)

    def head_qs(q):
        m0 = _head0_mask(q.shape)
        zero = jnp.zeros_like(q)
        return [jnp.where(m0, q, zero), jnp.where(m0, zero, q)]

    def merge(outs):
        return jnp.where(_head0_mask(outs[0].shape), outs[0], outs[1]).astype(BF16)

    lane = lax.broadcasted_iota(jnp.int32, (GRID_W, key_w), 1)
    for first_j, pat in zip((1, 2, n_blk), _bias_layout(n_blk)):
        @pl.when(j == first_j)
        def _(pat=pat):
            for hh in range(2):
                for m, (copy, base, dst) in enumerate(pat):
                    strip = tab_ref[hh, copy, :, base:base + key_w]
                    ok = jnp.logical_and(lane >= dst, lane < dst + NA_ROWS * GRID_W)
                    bias_ref[hh, m * GRID_W:(m + 1) * GRID_W, :] = jnp.where(ok, strip, NEG_INF)

    @pl.when(j == 0)
    def _():
        qs = head_qs(q_ref[0, 0:ctx_len, :])
        s = [lax.dot_general(qs[i], k_ctx, nt, preferred_element_type=F32) for i in nc]
        p = [jnp.exp(s[i] - jnp.max(s[i], axis=-1, keepdims=True)) for i in nc]
        o = [jnp.dot(p[i].astype(BF16), v_ctx, preferred_element_type=F32) for i in nc]
        o_ref[0, 0:ctx_len, :] = merge(
            [o[i] * (1.0 / jnp.sum(p[i], axis=-1, keepdims=True)) for i in nc])

    @pl.when(j > 0)
    def _():
        q_row0 = pl.multiple_of(ctx_len + (j - 1) * q_rows, GRID_W)
        key_row0 = jnp.clip(Q_ROWS * (j - 1) - NA_ROWS // 2, 0, n_blk * Q_ROWS - KEY_ROWS)
        row0 = pl.multiple_of(ctx_len + key_row0 * GRID_W, GRID_W)
        qs = head_qs(q_ref[0, pl.ds(q_row0, q_rows), :])
        k_loc = k_ref[0, pl.ds(row0, key_w), :]
        v_loc = v_ref[0, pl.ds(row0, key_w), :]
        s_loc = [lax.dot_general(qs[i], k_loc, nt, preferred_element_type=F32) + bias_ref[i]
                 for i in nc]
        s_ctx = [lax.dot_general(qs[i], k_ctx, nt, preferred_element_type=F32) for i in nc]
        mx = [jnp.maximum(jnp.max(s_loc[i], axis=-1, keepdims=True),
                          jnp.max(s_ctx[i], axis=-1, keepdims=True)) for i in nc]
        p_loc = [jnp.exp(s_loc[i] - mx[i]) for i in nc]
        p_ctx = [jnp.exp(s_ctx[i] - mx[i]) for i in nc]
        den = [jnp.sum(p_loc[i], axis=-1, keepdims=True) + jnp.sum(p_ctx[i], axis=-1, keepdims=True)
               for i in nc]
        o = [jnp.dot(p_loc[i].astype(BF16), v_loc, preferred_element_type=F32)
             + jnp.dot(p_ctx[i].astype(BF16), v_ctx, preferred_element_type=F32) for i in nc]
        o_ref[0, pl.ds(q_row0, q_rows), :] = merge([o[i] * (1.0 / den[i]) for i in nc])


def _attention(q, k, v, table, ctx_len):
    nb, tt, w = q.shape
    q_rows = Q_ROWS * GRID_W
    n_blk = (tt - ctx_len) // q_rows
    assert (tt - ctx_len) % q_rows == 0 and ctx_len % GRID_W == 0 and n_blk >= 3
    whole = pl.BlockSpec((1, tt, LANES), lambda hp, b, j: (b, 0, hp))
    return pl.pallas_call(
        functools.partial(_attn_kernel, ctx_len=ctx_len, n_blk=n_blk),
        grid=(w // LANES, nb, n_blk + 1),
        in_specs=[whole, whole, whole,
                  pl.BlockSpec((2, 2, GRID_W, _TAB_W), lambda hp, b, j: (hp, 0, 0, 0))],
        out_specs=whole,
        out_shape=jax.ShapeDtypeStruct((nb, tt, w), BF16),
        scratch_shapes=[pltpu.VMEM((2, q_rows, KEY_ROWS * GRID_W), F32)],
        compiler_params=_params("parallel", "parallel", "arbitrary"),
    )(q, k, v, table)


def _bias_tables(na_bias):
    n_layer, n_head, n_dr, n_dc = na_bias.shape
    cols = np.arange(GRID_W)
    col_start = np.clip(cols - NA_COLS // 2, 0, GRID_W - NA_COLS)
    in_window = (cols[None, :] >= col_start[:, None]) & (cols[None, :] < col_start[:, None] + NA_COLS)
    dc = np.clip(cols[None, :] - cols[:, None], -(NA_COLS - 1), NA_COLS - 1) + (NA_COLS - 1)
    onehot = (dc[None] == np.arange(n_dc)[:, None, None]).astype(np.float32)
    tiles = jnp.einsum('lhrj,jqk->lhqrk', na_bias.astype(F32), onehot, precision=HIGHEST)
    tiles = jnp.where(in_window[None, None, :, None, :], tiles, NEG_INF)
    strip = tiles.reshape(n_layer, n_head, GRID_W, n_dr * GRID_W)
    copies = []
    for shift in (0, HEAD_DIM):
        left = _TAB_PAD + shift
        copies.append(jnp.pad(strip, ((0, 0), (0, 0), (0, 0), (left, _TAB_W - left - n_dr * GRID_W)),
                              constant_values=NEG_INF))
    return jnp.stack(copies, axis=2)


def _prep_kernel(p_ref, hp_ref, hn_ref, mu_ref, w0_ref, w2_ref, a0_ref, a2_ref, g2_ref,
                 kk_ref, ka_ref, u_ref,
                 r_ref, v_ref, kkn_ref, g_ref, bon_ref, lw_ref, kd_ref, bb_ref, *, hw, n_tile):
    t = pl.program_id(1)
    p = p_ref[0]
    rows = lax.broadcasted_iota(jnp.int32, (8, 1), 0)
    prev_ok = (t >= 2).astype(F32)
    next_ok = jnp.logical_and(t >= 1, t <= n_tile - 2).astype(F32)
    prev = pltpu.roll(p, 1, 0)
    nxt = pltpu.roll(p, TM - 1, 0)
    prev = jnp.concatenate(
        [jnp.where(rows == 0, hp_ref[0, 7:8, :] * prev_ok, prev[0:8]), prev[8:]], axis=0)
    nxt = jnp.concatenate(
        [nxt[:TM - 8], jnp.where(rows == 7, hn_ref[0, 0:1, :] * next_ok, nxt[TM - 8:])], axis=0)
    p = p + mu_ref[...] * (0.5 * (prev + nxt) - p)
    r = p[:, 0:hw]
    k = p[:, hw:2 * hw]
    v = p[:, 2 * hw:3 * hw]
    lora = p[:, 3 * hw:3 * hw + LANES]
    dg = p[:, 3 * hw + LANES:3 * hw + 2 * LANES]
    g_ref[0] = _bdot(_sigmoid(dg), g2_ref[...])
    r_ref[0] = r
    v_ref[0] = v.astype(BF16)
    tl = jnp.tanh(lora)
    ksum = jnp.zeros_like(k)
    kks = []
    for col in range(hw // LANES):
        sl = slice(col * LANES, (col + 1) * LANES)
        kc = k[:, sl] * kk_ref[:, sl]
        kks.append(kc * lax.rsqrt(jnp.maximum(_head_sum(kc * kc), L2_EPS)))
    kkn = jnp.concatenate(kks, axis=1)
    kkn_ref[0] = kkn
    for d in range(2):
        z = w0_ref[d:d + 1, :] + _bdot(tl, w2_ref[d])
        lw_ref[d, 0] = -np.float32(np.exp(-0.5)) * _sigmoid(z)
        a = _sigmoid(a0_ref[d:d + 1, :] + _bdot(lora, a2_ref[d]))
        kd = k * (1.0 + (a - 1.0) * ka_ref[...])
        kd_ref[d, 0] = kd
        bb_ref[d, 0] = kkn * a
        ksum = ksum + kd
    bon = r * ksum * u_ref[...]
    for col in range(hw // LANES):
        sl = slice(col * LANES, (col + 1) * LANES)
        bon_ref[0, :, sl] = _head_sum(bon[:, sl]) * v[:, sl]


def _rwkv_prepare(pr, shift_mu, w0, w2p, a0, a2p, g2, key_k, key_a, bonus_u, hw):
    nb, tt, rw = pr.shape
    n_tile = tt // TM
    sub = TM // 8
    tok = jax.ShapeDtypeStruct((nb, tt, hw), F32)
    tok2 = jax.ShapeDtypeStruct((2, nb, tt, hw), F32)
    tok_spec = _row_specs(hw)
    tok2_spec = pl.BlockSpec((2, 1, TM, hw), lambda b, t: (0, b, t, 0))
    return pl.pallas_call(
        functools.partial(_prep_kernel, hw=hw, n_tile=n_tile),
        grid=(nb, n_tile),
        in_specs=[_row_specs(rw),
                  pl.BlockSpec((1, 8, rw), lambda b, t: (b, jnp.maximum(t * sub - 1, 0), 0)),
                  pl.BlockSpec((1, 8, rw), lambda b, t: (b, jnp.minimum((t + 1) * sub, tt // 8 - 1), 0)),
                  _const_spec((1, rw)), _const_spec((2, hw)), _const_spec((2, LANES, hw)),
                  _const_spec((2, hw)), _const_spec((2, LANES, hw)), _const_spec((LANES, hw)),
                  _const_spec((1, hw)), _const_spec((1, hw)), _const_spec((1, hw))],
        out_specs=[tok_spec] * 5 + [tok2_spec] * 3,
        out_shape=[tok, jax.ShapeDtypeStruct((nb, tt, hw), BF16)] + [tok] * 3 + [tok2] * 3,
        compiler_params=_params("parallel", "parallel"),
    )(pr, pr, pr, shift_mu, w0, w2p, a0, a2p, g2, key_k, key_a, bonus_u)


def _stack_heads(x):
    m0 = _head0_mask(x.shape)
    zero = jnp.zeros_like(x)
    return jnp.concatenate([jnp.where(m0, x, zero), jnp.where(m0, zero, x)], axis=0)


def _scan_kernel(rf_ref, vf_ref, kkf_ref, lwf_ref, kdf_ref, bbf_ref,
                 rb_ref, vb_ref, kkb_ref, lwb_ref, kdb_ref, bbb_ref,
                 of_ref, ob_ref, st_ref, *, hw):
    n_pair = hw // LANES
    c2 = 2 * CHUNK

    @pl.when(pl.program_id(1) == 0)
    def _():
        st_ref[...] = jnp.zeros(st_ref.shape, F32)

    ti = lax.broadcasted_iota(jnp.int32, (CHUNK, CHUNK), 0)
    tj = lax.broadcasted_iota(jnp.int32, (CHUNK, CHUNK), 1)
    ri = lax.broadcasted_iota(jnp.int32, (c2, c2), 0)
    rj = lax.broadcasted_iota(jnp.int32, (c2, c2), 1)
    same = (ri // CHUNK) == (rj // CHUNK)
    blk_r, blk_c = ri % CHUNK, rj % CHUNK
    eye = (ri == rj).astype(F32)
    li = lax.broadcasted_iota(jnp.int32, (LANES, LANES), 0)
    lj = lax.broadcasted_iota(jnp.int32, (LANES, LANES), 1)
    eye_l = (li == lj).astype(F32)
    sls = [slice(pr * LANES, (pr + 1) * LANES) for pr in range(n_pair)]

    ops = []
    for bi in range(rf_ref.shape[0]):
        for d, (r_ref, v_ref, kk_ref, lw_ref, kd_ref, bb_ref) in enumerate(
                ((rf_ref, vf_ref, kkf_ref, lwf_ref, kdf_ref, bbf_ref),
                 (rb_ref, vb_ref, kkb_ref, lwb_ref, kdb_ref, bbb_ref))):
            incl1 = (tj <= ti) if d == 0 else (tj >= ti)
            before = (blk_c - blk_r) if d == 0 else (blk_r - blk_c)
            incl = jnp.logical_and(same, before <= 0).astype(F32)
            strict = jnp.logical_and(same, before < 0).astype(F32)
            lw = lw_ref[0, bi]
            incl1 = incl1.astype(F32).astype(BF16)
            lw1 = lw.astype(BF16)
            lw2, lw3 = _split(lw - lw1.astype(F32))
            cum = (_mm(incl1, lw3) + _mm(incl1, lw2)) + _mm(incl1, lw1)
            cum_prev = cum - lw
            tot = jnp.sum(lw, axis=0, keepdims=True)
            mid = 0.5 * tot
            r, v, kk = r_ref[bi], v_ref[bi], kk_ref[bi]
            kd, bb = kd_ref[0, bi], bb_ref[0, bi]
            e_inv = jnp.exp(mid - cum)
            e_end = jnp.exp(tot - cum)
            full = dict(a_hat=kk * jnp.exp(cum_prev - mid), r_hat=r * jnp.exp(cum - mid),
                        k_til=kd * e_inv, b_til=bb * e_inv, v=v,
                        a_abs=kk * jnp.exp(cum_prev), r_abs=r * jnp.exp(cum),
                        k_end=kd * e_end, b_end=bb * e_end)
            e_tot = jnp.exp(tot)
            for p in range(n_pair):
                op = {name: _stack_heads(x[:, sls[p]]) for name, x in full.items()}
                op.update(incl=incl, strict=strict, e_tot=e_tot[:, sls[p]], bi=bi, d=d, p=p)
                ops.append(op)

    ch = range(len(ops))
    bf = lambda name: [ops[i][name].astype(BF16) for i in ch]
    a_s, r_s, k_s, b_s, v_s = bf("a_hat"), bf("r_hat"), bf("k_til"), bf("b_til"), bf("v")
    ar_s = [jnp.concatenate([a_s[i], r_s[i]], axis=0) for i in ch]
    ar_b = [_mm(ar_s[i], b_s[i], _NT) for i in ch]
    ar_k = [_mm(ar_s[i], k_s[i], _NT) for i in ch]
    low = [ops[i]["strict"] * ar_b[i][:c2] for i in ch]
    m_rb = [(ops[i]["incl"] * ar_b[i][c2:]).astype(BF16) for i in ch]
    m_kk = [jnp.concatenate([ops[i]["strict"] * ar_k[i][:c2], ops[i]["incl"] * ar_k[i][c2:]],
                            axis=0).astype(BF16) for i in ch]
    t_inv = None
    s = 1
    while s < CHUNK:
        join = jnp.logical_and(blk_r // (2 * s) == blk_c // (2 * s), blk_r // s != blk_c // s)
        join = join.astype(F32)
        if t_inv is None:
            t_inv = [eye - join * low[i] for i in ch]
        else:
            d2 = [t_inv[i].astype(BF16) for i in ch]
            ed = [_mm((join * low[i]).astype(BF16), d2[i]).astype(BF16) for i in ch]
            t_inv = [t_inv[i] - _mm(d2[i], ed[i]) for i in ch]
        s *= 2
    mv = [_mm(m_kk[i], v_s[i]) for i in ch]
    au = [_mm(t_inv[i].astype(BF16),
              jnp.concatenate([ops[i]["a_abs"], mv[i][:c2]], axis=1).astype(BF16)).astype(BF16)
          for i in ch]
    qo = [jnp.concatenate([ops[i]["r_abs"], mv[i][c2:]], axis=1) - _mm(m_rb[i], au[i]) for i in ch]
    ba = [_mm(ops[i]["b_end"].astype(BF16), au[i], _TN) for i in ch]
    kv = [_mm(ops[i]["k_end"].astype(BF16), v_s[i], _TN) for i in ch]
    where = [(ops[i]["bi"], ops[i]["d"], ops[i]["p"]) for i in ch]
    st = [_split(st_ref[where[i]]) for i in ch]
    g2 = [_split(eye_l * ops[i]["e_tot"] - ba[i][:, :LANES]) for i in ch]
    prod = [_mm(jnp.concatenate([qo[i][:, :LANES].astype(BF16), g2[i][0], g2[i][1]], axis=0),
                st[i][0]) for i in ch]
    g_lo = [_mm(g2[i][0], st[i][1]) for i in ch]
    for i in ch:
        bi, d, p = where[i]
        o_s = prod[i][:c2] + qo[i][:, LANES:]
        o_ref = of_ref if d == 0 else ob_ref
        o_ref[bi, :, sls[p]] = o_s[:CHUNK] + o_s[CHUNK:]
    for i in ch:
        st_ref[where[i]] = ((g_lo[i] + prod[i][c2 + LANES:]) + prod[i][c2:c2 + LANES]
                            + (kv[i] - ba[i][:, LANES:]))


def _rwkv_scan(r, v, kk, lw, kd, bb, ctx_len):
    nb, tt, hw = r.shape
    n_chunk = tt // CHUNK
    n_ctx = ctx_len // CHUNK

    def back(c):
        return jnp.where(c < n_ctx, n_ctx - 1 - c, n_chunk + n_ctx - 1 - c)

    sb = SCAN_BATCH
    assert nb % sb == 0
    fwd = pl.BlockSpec((sb, CHUNK, hw), lambda b, c: (b, c, 0))
    bwd = pl.BlockSpec((sb, CHUNK, hw), lambda b, c: (b, back(c), 0))
    fwd_d = pl.BlockSpec((1, sb, CHUNK, hw), lambda b, c: (0, b, c, 0))
    bwd_d = pl.BlockSpec((1, sb, CHUNK, hw), lambda b, c: (1, b, back(c), 0))
    return pl.pallas_call(
        functools.partial(_scan_kernel, hw=hw),
        grid=(nb // sb, n_chunk),
        in_specs=[fwd, fwd, fwd, fwd_d, fwd_d, fwd_d, bwd, bwd, bwd, bwd_d, bwd_d, bwd_d],
        out_specs=[fwd, bwd],
        out_shape=[jax.ShapeDtypeStruct((nb, tt, hw), F32)] * 2,
        scratch_shapes=[pltpu.VMEM((sb, 2, hw // LANES, LANES, LANES), F32)],
        compiler_params=_params("parallel", "arbitrary"),
    )(r, v, kk, lw, kd, bb, r, v, kk, lw, kd, bb)


def _back_kernel(x_ref, mod_ref, gain_ref, attn_ref, of_ref, ob_ref, bon_ref, g_ref, lg_ref, lb_ref,
                 w_ref, wup_ref, wdown_ref, o_ref, *, hw):
    mod = mod_ref[0, 0]
    o = of_ref[0] + ob_ref[0]
    cols = []
    for col in range(hw // LANES):
        sl = slice(col * LANES, (col + 1) * LANES)
        oc = o[:, sl]
        dev = oc - _head_sum(oc) * (1.0 / HEAD_DIM)
        var = _head_sum(dev * dev) * (1.0 / HEAD_DIM)
        cols.append(dev * lax.rsqrt(var + LNX_EPS))
    on = jnp.concatenate(cols, axis=1) * lg_ref[...] + lb_ref[...]
    rw = ((on + bon_ref[0]) * g_ref[0]).astype(BF16)
    aw = attn_ref.shape[2]
    y = (jnp.dot(attn_ref[0], w_ref[0:aw, :], preferred_element_type=F32)
         + jnp.dot(rw, w_ref[aw:, :], preferred_element_type=F32))
    x = x_ref[0] + mod[5:6] * y
    o_ref[0] = _ffn_apply(x, mod, 2, gain_ref[2:3], wup_ref, wdown_ref, FFN_CHUNKS)


def _back(x, modtab, gains, attn, o_fwd, o_bwd, bon, g, lnx_gain, lnx_bias, w_out, wup, wdown,
          layer, first_tile):
    nb, tt, d = x.shape
    hw = bon.shape[2]
    aw = attn.shape[2]
    n_tile = tt // TM - first_tile
    rows = lambda w: _row_specs(w, first_tile)
    return pl.pallas_call(
        functools.partial(_back_kernel, hw=hw),
        grid=(nb, n_tile),
        in_specs=[rows(d), _mod_spec(d, first_tile), _const_spec((3, d)), rows(aw), rows(hw),
                  rows(hw), rows(hw), rows(hw), _const_spec((1, hw)), _const_spec((1, hw)),
                  _stack_spec(w_out, layer), _stack_spec(wup, layer, 1),
                  _stack_spec(wdown, layer, 1)],
        out_specs=_row_specs(d),
        out_shape=jax.ShapeDtypeStruct((nb, n_tile * TM, d), F32),
        compiler_params=_params("parallel", "parallel"),
    )(x, modtab, gains, attn, o_fwd, o_bwd, bon, g, lnx_gain, lnx_bias, w_out, wup, wdown)


def kernel(x, c, ctx, c_ctx, w_mod, b_mod, norm_gain, ffn_up, ffn_down, w_in, q_gain, k_gain,
           na_bias, shift_mu, decay_w0, decay_w2, iclr_a0, iclr_a2, gate_g2, key_k, key_a,
           bonus_u, lnx_gain, lnx_bias, w_out):
    nb, seq, d = x.shape
    ctx_len = ctx.shape[1]
    depth = w_mod.shape[0]
    n_heads = na_bias.shape[1]
    attn_w = n_heads * HEAD_DIM
    hw = decay_w0.shape[2]
    lora_w = decay_w2.shape[2]
    assert ctx_len == TM and seq % (Q_ROWS * GRID_W) == 0 and attn_w % LANES == 0 and hw % LANES == 0
    assert lora_w + iclr_a2.shape[2] == LANES and gate_g2.shape[1] == LANES and nb + 1 <= 8

    cvec = jnp.zeros((8, d), F32).at[:nb].set(c).at[nb].set(c_ctx)
    mods = _modulation(cvec, w_mod, b_mod).reshape(depth, 8, 9, d)
    modtab = jnp.stack([jnp.broadcast_to(mods[:, nb:nb + 1], (depth, nb, 9, d)), mods[:, :nb]], axis=2)

    up_b, down_b = ffn_up.astype(BF16), ffn_down.astype(BF16)
    w_in_b, w_out_b = w_in.astype(BF16), w_out.astype(BF16)
    zpad = jnp.zeros_like(decay_w2)
    w2p = jnp.concatenate([decay_w2, zpad], axis=2)
    a2p = jnp.concatenate([zpad, iclr_a2], axis=2)
    bias_tabs = _bias_tables(na_bias)

    xs = jnp.concatenate([ctx, x], axis=1)
    for l in range(depth):
        mt = modtab[l]
        xs, q, k, v, pr = _front(xs, mt, norm_gain[l], up_b, down_b, w_in_b, l,
                                 jnp.tile(q_gain[l], 2)[None], jnp.tile(k_gain[l], 2)[None], attn_w)
        attn = _attention(q, k, v, bias_tabs[l], ctx_len)
        r, vr, kk, g, bon, lw, kd, bb = _rwkv_prepare(
            pr, shift_mu[l][None], decay_w0[l], w2p[l], iclr_a0[l], a2p[l], gate_g2[l],
            key_k[l][None], key_a[l][None], bonus_u[l].reshape(1, hw), hw)
        o_fwd, o_bwd = _rwkv_scan(r, vr, kk, lw, kd, bb, ctx_len)
        xs = _back(xs, mt, norm_gain[l], attn, o_fwd, o_bwd, bon, g, lnx_gain[l][None],
                   lnx_bias[l][None], w_out_b, up_b, down_b, l,
                   first_tile=ctx_len // TM if l == depth - 1 else 0)
    return xs
```
